```python
import math
import jax, jax.numpy as jnp
from jax import lax
import numpy as np

D_MODEL = 2048
BATCH = 16
SEQ = 2048
DEPTH = 4
DEC_BATCH = 1
DEC_SEQ = 16384
PAST_LEN = 128

LRU_WIDTH = 768
LRU_BLOCKS = 6
LRU_BLOCK = LRU_WIDTH // LRU_BLOCKS
CONV_W = 4
LRU_C = 8.0
NA_HEADS = 12
NA_HEAD_DIM = 64
NA_WIDTH = NA_HEADS * NA_HEAD_DIM
GRID_W = 64
WIN_ROWS_MAX = 8
WIN_COLS = 16
COL_BLOCK = 16
KEY_COLS = 32
S5_WIDTH = 512
S5_GROUP = 16
S5_GROUPS = S5_WIDTH // S5_GROUP
S5_STATE = 64
MIX_WIDTH = LRU_WIDTH + NA_WIDTH + S5_WIDTH
IN_WIDTH = 2 * LRU_WIDTH + 3 * NA_WIDTH + S5_WIDTH
N_EXPERTS = 16
EXPERT_FF = 4096
EC_CAPACITY = 2
EPS = 1e-6

kernel_name = 'hybrid_bidir_lru_natten_s5_ec'


def rmsnorm(x, g):
    xf = x.astype(jnp.float32)
    y = xf * lax.rsqrt(jnp.mean(xf * xf, axis=-1, keepdims=True) + EPS)
    return (y * g.astype(jnp.float32)).astype(x.dtype)


def _linear_combine(e1, e2):
    a1, b1 = e1
    a2, b2 = e2
    return a1 * a2, a2 * b1 + b2


def _complex_combine(e1, e2):
    ar1, ai1, br1, bi1 = e1
    ar2, ai2, br2, bi2 = e2
    ar = ar2 * ar1 - ai2 * ai1
    ai = ar2 * ai1 + ai2 * ar1
    br = ar2 * br1 - ai2 * bi1 + br2
    bi = ar2 * bi1 + ai2 * br1 + bi2
    return ar, ai, br, bi


def rglru_mixer(xa, ga, conv_w, conv_b, wa, ba, wx, bx, lam):
    f32 = jnp.float32
    Bn, L, W = xa.shape
    xc = lax.conv_general_dilated(
        xa, conv_w[:, None, :].astype(xa.dtype), window_strides=(1,),
        padding=[(CONV_W // 2, CONV_W - 1 - CONV_W // 2)],
        dimension_numbers=('NWC', 'WIO', 'NWC'), feature_group_count=W) + conv_b
    xf = xc.astype(f32)
    xb = xf.reshape(Bn, L, LRU_BLOCKS, LRU_BLOCK)
    h_sum = jnp.zeros_like(xf)
    for d, rev in ((0, False), (1, True)):
        r = jax.nn.sigmoid(jnp.einsum('blhi,hij->blhj', xb, wa[d].astype(f32)).reshape(Bn, L, W) + ba[d].astype(f32))
        i = jax.nn.sigmoid(jnp.einsum('blhi,hij->blhj', xb, wx[d].astype(f32)).reshape(Bn, L, W) + bx[d].astype(f32))
        log_a = -LRU_C * r * jax.nn.softplus(-lam[d].astype(f32))
        a = jnp.exp(log_a)
        b = jnp.sqrt(-jnp.expm1(2.0 * log_a)) * (i * xf)
        _, h = lax.associative_scan(_linear_combine, (a, b), reverse=rev, axis=1)
        h_sum = h_sum + h
    return (h_sum * jax.nn.gelu(ga.astype(f32))).astype(xa.dtype)


def _na_static():
    j = np.arange(GRID_W // COL_BLOCK)
    cb_start = np.clip(j * COL_BLOCK - WIN_COLS // 2, 0, GRID_W - KEY_COLS)
    key_col = cb_start[:, None] + np.arange(KEY_COLS)
    q_col = j[:, None] * COL_BLOCK + np.arange(COL_BLOCK)
    cs = np.clip(q_col - WIN_COLS // 2, 0, GRID_W - WIN_COLS)
    kc = key_col[:, None, :]
    col_mask = (kc >= cs[..., None]) & (kc < cs[..., None] + WIN_COLS)
    dc_idx = np.clip(kc - q_col[..., None] + WIN_COLS - 1, 0, 2 * WIN_COLS - 2)
    return key_col, col_mask, dc_idx


def neighbourhood_attention(q, k, v, rpb):
    f32 = jnp.float32
    Bn, L, _ = q.shape
    rows = L // GRID_W
    kh = min(WIN_ROWS_MAX, rows)
    ncb = GRID_W // COL_BLOCK
    key_col, col_mask, dc_idx = _na_static()
    qg = (q * NA_HEAD_DIM ** -0.5).reshape(Bn, rows, GRID_W, NA_HEADS, NA_HEAD_DIM)
    kg = k.reshape(Bn, rows, GRID_W, NA_HEADS, NA_HEAD_DIM)
    vg = v.reshape(Bn, rows, GRID_W, NA_HEADS, NA_HEAD_DIM)
    rpb_cols = jnp.take(rpb.astype(f32), jnp.asarray(dc_idx), axis=2)
    mask = jnp.asarray(col_mask)[:, :, None, :]

    def row_fn(r):
        rs = jnp.clip(r - kh // 2, 0, rows - kh)
        qr = lax.dynamic_index_in_dim(qg, r, axis=1, keepdims=False).reshape(Bn, ncb, COL_BLOCK, NA_HEADS, NA_HEAD_DIM)
        kr = lax.dynamic_slice_in_dim(kg, rs, kh, axis=1)[:, :, key_col]
        vr = lax.dynamic_slice_in_dim(vg, rs, kh, axis=1)[:, :, key_col]
        s = jnp.einsum('bjuhd,bijvhd->bhjuiv', qr, kr).astype(f32)
        dr_idx = rs + jnp.arange(kh) - r + WIN_ROWS_MAX - 1
        bias = jnp.take(rpb_cols, dr_idx, axis=1).transpose(0, 2, 3, 1, 4)
        s = jnp.where(mask, s + bias[None], -1e30)
        p = jax.nn.softmax(s.reshape(Bn, NA_HEADS, ncb, COL_BLOCK, kh * KEY_COLS), axis=-1).reshape(s.shape)
        o = jnp.einsum('bhjuiv,bijvhd->bjuhd', p.astype(vr.dtype), vr)
        return o.reshape(Bn, GRID_W, NA_WIDTH)

    out = lax.map(row_fn, jnp.arange(rows))
    return out.transpose(1, 0, 2, 3).reshape(Bn, L, NA_WIDTH)


def s5_mixer(u, lam_re, lam_im, log_dt, b_re, b_im, c_re, c_im, d_skip, glu_w, glu_b):
    f32 = jnp.float32
    Bn, L, W = u.shape
    uf = u.astype(f32).reshape(Bn, L, S5_GROUPS, S5_GROUP)
    y = uf * d_skip.astype(f32).reshape(S5_GROUPS, S5_GROUP)
    for d, rev in ((0, False), (1, True)):
        lr = jnp.minimum(lam_re[d].astype(f32), -1e-4)
        li = lam_im[d].astype(f32)
        dt = jnp.exp(log_dt[d].astype(f32))[:, None]
        mag = jnp.exp(lr * dt)
        ar = mag * jnp.cos(li * dt)
        ai = mag * jnp.sin(li * dt)
        den = lr * lr + li * li
        nr = ar - 1.0
        fr = (nr * lr + ai * li) / den
        fi = (ai * lr - nr * li) / den
        br_d = b_re[d].astype(f32)
        bi_d = b_im[d].astype(f32)
        bbr = fr[..., None] * br_d - fi[..., None] * bi_d
        bbi = fr[..., None] * bi_d + fi[..., None] * br_d
        bur = jnp.einsum('blgc,gpc->blgp', uf, bbr)
        bui = jnp.einsum('blgc,gpc->blgp', uf, bbi)
        arb = jnp.broadcast_to(ar, bur.shape)
        aib = jnp.broadcast_to(ai, bur.shape)
        _, _, hr, hi = lax.associative_scan(_complex_combine, (arb, aib, bur, bui), reverse=rev, axis=1)
        y = y + jnp.einsum('blgp,gcp->blgc', hr, c_re[d].astype(f32)) - jnp.einsum('blgp,gcp->blgc', hi, c_im[d].astype(f32))
    z = jax.nn.gelu(y.reshape(Bn, L, W))
    zz = z @ glu_w.astype(f32) + glu_b.astype(f32)
    out = zz[..., :W] * jax.nn.sigmoid(zz[..., W:])
    return out.astype(u.dtype)


def expert_choice_ffn(x, router_w, w_gate, w_up, w_down):
    Bn, L, D = x.shape
    n = Bn * L
    cap = max(1, EC_CAPACITY * n // N_EXPERTS)
    xf = x.reshape(n, D)
    aff = jax.nn.softmax((xf @ router_w).astype(jnp.float32), axis=-1)
    g, idx = lax.top_k(aff.T, cap)
    xe = xf[idx]
    hdn = jax.nn.silu(jnp.einsum('ecd,edf->ecf', xe, w_gate)) * jnp.einsum('ecd,edf->ecf', xe, w_up)
    ye = jnp.einsum('ecf,efd->ecd', hdn, w_down) * g[..., None].astype(x.dtype)
    out = jnp.zeros_like(xf).at[idx.reshape(-1)].add(ye.reshape(-1, D))
    return out.reshape(Bn, L, D)


def trunk_layer(x, norm1_g, w_in, conv_w, conv_b, gate_a_w, gate_a_b, gate_x_w, gate_x_b, lru_lambda, rpb,
                s5_lambda_re, s5_lambda_im, s5_log_dt, s5_b_re, s5_b_im, s5_c_re, s5_c_im, s5_d, glu_w, glu_b,
                branch_g, w_out, norm2_g, router_w, w_gate, w_up, w_down):
    h = rmsnorm(x, norm1_g)
    proj = h @ w_in
    o1 = LRU_WIDTH
    o2 = o1 + LRU_WIDTH
    o3 = o2 + NA_WIDTH
    o4 = o3 + NA_WIDTH
    o5 = o4 + NA_WIDTH
    ya = rglru_mixer(proj[..., :o1], proj[..., o1:o2], conv_w, conv_b, gate_a_w, gate_a_b, gate_x_w, gate_x_b, lru_lambda)
    yb = neighbourhood_attention(proj[..., o2:o3], proj[..., o3:o4], proj[..., o4:o5], rpb)
    yc = s5_mixer(proj[..., o5:], s5_lambda_re, s5_lambda_im, s5_log_dt, s5_b_re, s5_b_im, s5_c_re, s5_c_im, s5_d, glu_w, glu_b)
    mixed = jnp.concatenate([
        rmsnorm(ya, branch_g[:LRU_WIDTH]),
        rmsnorm(yb, branch_g[LRU_WIDTH:LRU_WIDTH + NA_WIDTH]),
        rmsnorm(yc, branch_g[LRU_WIDTH + NA_WIDTH:])], axis=-1)
    x = x + mixed @ w_out
    x = x + expert_choice_ffn(rmsnorm(x, norm2_g), router_w, w_gate, w_up, w_down)
    return x


def setup_inputs(seed: int = 0) -> dict:
    key = jax.random.key(seed)
    ks = jax.random.split(key, 32)
    f32 = jnp.float32

    def nrm(k, shape, scale):
        return jax.random.normal(k, shape, f32) * scale

    x_prompt = nrm(ks[0], (BATCH, SEQ, D_MODEL), 1.0)
    x_sample = nrm(ks[1], (DEC_BATCH, DEC_SEQ, D_MODEL), 1.0)
    norm1_g = 1.0 + nrm(ks[2], (DEPTH, D_MODEL), 0.02)
    w_in = nrm(ks[3], (DEPTH, D_MODEL, IN_WIDTH), D_MODEL ** -0.5)
    conv_w = nrm(ks[4], (DEPTH, CONV_W, LRU_WIDTH), CONV_W ** -0.5)
    conv_b = nrm(ks[5], (DEPTH, LRU_WIDTH), 0.02)
    gate_a_w = nrm(ks[6], (DEPTH, 2, LRU_BLOCKS, LRU_BLOCK, LRU_BLOCK), LRU_BLOCK ** -0.5)
    gate_a_b = nrm(ks[7], (DEPTH, 2, LRU_WIDTH), 0.02)
    gate_x_w = nrm(ks[8], (DEPTH, 2, LRU_BLOCKS, LRU_BLOCK, LRU_BLOCK), LRU_BLOCK ** -0.5)
    gate_x_b = nrm(ks[9], (DEPTH, 2, LRU_WIDTH), 0.02)
    a_c = jax.random.uniform(ks[10], (DEPTH, 2, LRU_WIDTH), f32, 0.9, 0.999)
    sg = a_c ** (1.0 / LRU_C)
    lru_lambda = jnp.log(sg) - jnp.log1p(-sg)
    rpb = nrm(ks[11], (DEPTH, NA_HEADS, 2 * WIN_ROWS_MAX - 1, 2 * WIN_COLS - 1), 0.02)
    s5_lambda_re = -0.5 + nrm(ks[12], (DEPTH, 2, S5_GROUPS, S5_STATE), 0.01)
    s5_lambda_im = math.pi * jnp.arange(S5_STATE, dtype=f32) + nrm(ks[13], (DEPTH, 2, S5_GROUPS, S5_STATE), 0.01)
    s5_log_dt = jax.random.uniform(ks[14], (DEPTH, 2, S5_GROUPS), f32, math.log(1e-3), math.log(1e-1))
    s5_b_re = nrm(ks[15], (DEPTH, 2, S5_GROUPS, S5_STATE, S5_GROUP), (2 * S5_GROUP) ** -0.5)
    s5_b_im = nrm(ks[16], (DEPTH, 2, S5_GROUPS, S5_STATE, S5_GROUP), (2 * S5_GROUP) ** -0.5)
    s5_c_re = nrm(ks[17], (DEPTH, 2, S5_GROUPS, S5_GROUP, S5_STATE), (2 * S5_STATE) ** -0.5)
    s5_c_im = nrm(ks[18], (DEPTH, 2, S5_GROUPS, S5_GROUP, S5_STATE), (2 * S5_STATE) ** -0.5)
    s5_d = nrm(ks[19], (DEPTH, S5_WIDTH), 1.0)
    glu_w = nrm(ks[20], (DEPTH, S5_WIDTH, 2 * S5_WIDTH), S5_WIDTH ** -0.5)
    glu_b = nrm(ks[21], (DEPTH, 2 * S5_WIDTH), 0.02)
    branch_g = 1.0 + nrm(ks[22], (DEPTH, MIX_WIDTH), 0.02)
    w_out = nrm(ks[23], (DEPTH, MIX_WIDTH, D_MODEL), MIX_WIDTH ** -0.5)
    norm2_g = 1.0 + nrm(ks[24], (DEPTH, D_MODEL), 0.02)
    router_w = nrm(ks[25], (DEPTH, D_MODEL, N_EXPERTS), D_MODEL ** -0.5)
    w_gate = nrm(ks[26], (DEPTH, N_EXPERTS, D_MODEL, EXPERT_FF), D_MODEL ** -0.5)
    w_up = nrm(ks[27], (DEPTH, N_EXPERTS, D_MODEL, EXPERT_FF), D_MODEL ** -0.5)
    w_down = nrm(ks[28], (DEPTH, N_EXPERTS, EXPERT_FF, D_MODEL), EXPERT_FF ** -0.5)
    final_g = 1.0 + nrm(ks[29], (D_MODEL,), 0.02)
    return {'x_prompt': x_prompt, 'x_sample': x_sample, 'norm1_g': norm1_g, 'w_in': w_in,
            'conv_w': conv_w, 'conv_b': conv_b, 'gate_a_w': gate_a_w, 'gate_a_b': gate_a_b,
            'gate_x_w': gate_x_w, 'gate_x_b': gate_x_b, 'lru_lambda': lru_lambda, 'rpb': rpb,
            's5_lambda_re': s5_lambda_re, 's5_lambda_im': s5_lambda_im, 's5_log_dt': s5_log_dt,
            's5_b_re': s5_b_re, 's5_b_im': s5_b_im, 's5_c_re': s5_c_re, 's5_c_im': s5_c_im,
            's5_d': s5_d, 'glu_w': glu_w, 'glu_b': glu_b, 'branch_g': branch_g, 'w_out': w_out,
            'norm2_g': norm2_g, 'router_w': router_w, 'w_gate': w_gate, 'w_up': w_up,
            'w_down': w_down, 'final_g': final_g}


def reference(x_prompt, x_sample, norm1_g, w_in, conv_w, conv_b, gate_a_w, gate_a_b, gate_x_w, gate_x_b,
              lru_lambda, rpb, s5_lambda_re, s5_lambda_im, s5_log_dt, s5_b_re, s5_b_im, s5_c_re, s5_c_im,
              s5_d, glu_w, glu_b, branch_g, w_out, norm2_g, router_w, w_gate, w_up, w_down, final_g):
    xp = x_prompt
    xs = x_sample
    for l in range(DEPTH):
        lp = (norm1_g[l], w_in[l], conv_w[l], conv_b[l], gate_a_w[l], gate_a_b[l], gate_x_w[l], gate_x_b[l],
              lru_lambda[l], rpb[l], s5_lambda_re[l], s5_lambda_im[l], s5_log_dt[l], s5_b_re[l], s5_b_im[l],
              s5_c_re[l], s5_c_im[l], s5_d[l], glu_w[l], glu_b[l], branch_g[l], w_out[l], norm2_g[l],
              router_w[l], w_gate[l], w_up[l], w_down[l])
        xp = trunk_layer(xp, *lp)
        xs = trunk_layer(xs, *lp)
    y_prompt = rmsnorm(xp, final_g)
    y_sample = rmsnorm(xs, final_g)
    return (y_prompt, y_sample)
```

```python
import functools
import math

import jax
import jax.numpy as jnp
import numpy as np
from jax import lax
from jax.experimental import pallas as pl
from jax.experimental.pallas import tpu as pltpu

D_MODEL = 2048
DEPTH = 4

LRU_WIDTH = 768
LRU_BLOCKS = 6
LRU_BLOCK = LRU_WIDTH // LRU_BLOCKS
CONV_W = 4
LRU_C = 8.0
NA_HEADS = 12
NA_HEAD_DIM = 64
NA_WIDTH = NA_HEADS * NA_HEAD_DIM
GRID_W = 64
WIN_ROWS_MAX = 8
WIN_COLS = 16
COL_BLOCK = 16
KEY_COLS = 32
S5_WIDTH = 512
S5_GROUP = 16
S5_GROUPS = S5_WIDTH // S5_GROUP
S5_STATE = 64
MIX_WIDTH = LRU_WIDTH + NA_WIDTH + S5_WIDTH
IN_WIDTH = 2 * LRU_WIDTH + 3 * NA_WIDTH + S5_WIDTH
N_EXPERTS = 16
EXPERT_FF = 4096
EC_CAPACITY = 2
EPS = 1e-6


def rmsnorm(x, g):
    xf = x.astype(jnp.float32)
    y = xf * lax.rsqrt(jnp.mean(xf * xf, axis=-1, keepdims=True) + EPS)
    return (y * g.astype(jnp.float32)).astype(x.dtype)


def _linear_combine(e1, e2):
    a1, b1 = e1
    a2, b2 = e2
    return a1 * a2, a2 * b1 + b2


def _complex_combine(e1, e2):
    ar1, ai1, br1, bi1 = e1
    ar2, ai2, br2, bi2 = e2
    ar = ar2 * ar1 - ai2 * ai1
    ai = ar2 * ai1 + ai2 * ar1
    br = ar2 * br1 - ai2 * bi1 + br2
    bi = ar2 * bi1 + ai2 * br1 + bi2
    return ar, ai, br, bi


def rglru_mixer(xa, ga, conv_w, conv_b, wa, ba, wx, bx, lam):
    f32 = jnp.float32
    Bn, L, W = xa.shape
    xc = lax.conv_general_dilated(
        xa, conv_w[:, None, :].astype(xa.dtype), window_strides=(1,),
        padding=[(CONV_W // 2, CONV_W - 1 - CONV_W // 2)],
        dimension_numbers=('NWC', 'WIO', 'NWC'), feature_group_count=W) + conv_b
    xf = xc.astype(f32)
    xb = xf.reshape(Bn, L, LRU_BLOCKS, LRU_BLOCK)
    h_sum = jnp.zeros_like(xf)
    for d, rev in ((0, False), (1, True)):
        r = jax.nn.sigmoid(jnp.einsum('blhi,hij->blhj', xb, wa[d].astype(f32)).reshape(Bn, L, W) + ba[d].astype(f32))
        i = jax.nn.sigmoid(jnp.einsum('blhi,hij->blhj', xb, wx[d].astype(f32)).reshape(Bn, L, W) + bx[d].astype(f32))
        log_a = -LRU_C * r * jax.nn.softplus(-lam[d].astype(f32))
        a = jnp.exp(log_a)
        b = jnp.sqrt(-jnp.expm1(2.0 * log_a)) * (i * xf)
        _, h = lax.associative_scan(_linear_combine, (a, b), reverse=rev, axis=1)
        h_sum = h_sum + h
    return (h_sum * jax.nn.gelu(ga.astype(f32))).astype(xa.dtype)


def _na_static():
    j = np.arange(GRID_W // COL_BLOCK)
    cb_start = np.clip(j * COL_BLOCK - WIN_COLS // 2, 0, GRID_W - KEY_COLS)
    key_col = cb_start[:, None] + np.arange(KEY_COLS)
    q_col = j[:, None] * COL_BLOCK + np.arange(COL_BLOCK)
    cs = np.clip(q_col - WIN_COLS // 2, 0, GRID_W - WIN_COLS)
    kc = key_col[:, None, :]
    col_mask = (kc >= cs[..., None]) & (kc < cs[..., None] + WIN_COLS)
    dc_idx = np.clip(kc - q_col[..., None] + WIN_COLS - 1, 0, 2 * WIN_COLS - 2)
    return key_col, col_mask, dc_idx


def neighbourhood_attention(q, k, v, rpb):
    f32 = jnp.float32
    Bn, L, _ = q.shape
    rows = L // GRID_W
    kh = min(WIN_ROWS_MAX, rows)
    ncb = GRID_W // COL_BLOCK
    key_col, col_mask, dc_idx = _na_static()
    qg = (q * NA_HEAD_DIM ** -0.5).reshape(Bn, rows, GRID_W, NA_HEADS, NA_HEAD_DIM)
    kg = k.reshape(Bn, rows, GRID_W, NA_HEADS, NA_HEAD_DIM)
    vg = v.reshape(Bn, rows, GRID_W, NA_HEADS, NA_HEAD_DIM)
    rpb_cols = jnp.take(rpb.astype(f32), jnp.asarray(dc_idx), axis=2)
    mask = jnp.asarray(col_mask)[:, :, None, :]

    def row_fn(r):
        rs = jnp.clip(r - kh // 2, 0, rows - kh)
        qr = lax.dynamic_index_in_dim(qg, r, axis=1, keepdims=False).reshape(Bn, ncb, COL_BLOCK, NA_HEADS, NA_HEAD_DIM)
        kr = lax.dynamic_slice_in_dim(kg, rs, kh, axis=1)[:, :, key_col]
        vr = lax.dynamic_slice_in_dim(vg, rs, kh, axis=1)[:, :, key_col]
        s = jnp.einsum('bjuhd,bijvhd->bhjuiv', qr, kr).astype(f32)
        dr_idx = rs + jnp.arange(kh) - r + WIN_ROWS_MAX - 1
        bias = jnp.take(rpb_cols, dr_idx, axis=1).transpose(0, 2, 3, 1, 4)
        s = jnp.where(mask, s + bias[None], -1e30)
        p = jax.nn.softmax(s.reshape(Bn, NA_HEADS, ncb, COL_BLOCK, kh * KEY_COLS), axis=-1).reshape(s.shape)
        o = jnp.einsum('bhjuiv,bijvhd->bjuhd', p.astype(vr.dtype), vr)
        return o.reshape(Bn, GRID_W, NA_WIDTH)

    out = lax.map(row_fn, jnp.arange(rows))
    return out.transpose(1, 0, 2, 3).reshape(Bn, L, NA_WIDTH)


def s5_mixer(u, lam_re, lam_im, log_dt, b_re, b_im, c_re, c_im, d_skip, glu_w, glu_b):
    f32 = jnp.float32
    Bn, L, W = u.shape
    uf = u.astype(f32).reshape(Bn, L, S5_GROUPS, S5_GROUP)
    y = uf * d_skip.astype(f32).reshape(S5_GROUPS, S5_GROUP)
    for d, rev in ((0, False), (1, True)):
        lr = jnp.minimum(lam_re[d].astype(f32), -1e-4)
        li = lam_im[d].astype(f32)
        dt = jnp.exp(log_dt[d].astype(f32))[:, None]
        mag = jnp.exp(lr * dt)
        ar = mag * jnp.cos(li * dt)
        ai = mag * jnp.sin(li * dt)
        den = lr * lr + li * li
        nr = ar - 1.0
        fr = (nr * lr + ai * li) / den
        fi = (ai * lr - nr * li) / den
        br_d = b_re[d].astype(f32)
        bi_d = b_im[d].astype(f32)
        bbr = fr[..., None] * br_d - fi[..., None] * bi_d
        bbi = fr[..., None] * bi_d + fi[..., None] * br_d
        bur = jnp.einsum('blgc,gpc->blgp', uf, bbr)
        bui = jnp.einsum('blgc,gpc->blgp', uf, bbi)
        arb = jnp.broadcast_to(ar, bur.shape)
        aib = jnp.broadcast_to(ai, bur.shape)
        _, _, hr, hi = lax.associative_scan(_complex_combine, (arb, aib, bur, bui), reverse=rev, axis=1)
        y = y + jnp.einsum('blgp,gcp->blgc', hr, c_re[d].astype(f32)) - jnp.einsum('blgp,gcp->blgc', hi, c_im[d].astype(f32))
    z = jax.nn.gelu(y.reshape(Bn, L, W))
    zz = z @ glu_w.astype(f32) + glu_b.astype(f32)
    out = zz[..., :W] * jax.nn.sigmoid(zz[..., W:])
    return out.astype(u.dtype)


def expert_choice_ffn(x, router_w, w_gate, w_up, w_down):
    Bn, L, D = x.shape
    n = Bn * L
    cap = max(1, EC_CAPACITY * n // N_EXPERTS)
    xf = x.reshape(n, D)
    aff = jax.nn.softmax((xf @ router_w).astype(jnp.float32), axis=-1)
    g, idx = lax.top_k(aff.T, cap)
    xe = xf[idx]
    hdn = jax.nn.silu(jnp.einsum('ecd,edf->ecf', xe, w_gate)) * jnp.einsum('ecd,edf->ecf', xe, w_up)
    ye = jnp.einsum('ecf,efd->ecd', hdn, w_down) * g[..., None].astype(x.dtype)
    out = jnp.zeros_like(xf).at[idx.reshape(-1)].add(ye.reshape(-1, D))
    return out.reshape(Bn, L, D)


def trunk_layer(x, norm1_g, w_in, conv_w, conv_b, gate_a_w, gate_a_b, gate_x_w, gate_x_b, lru_lambda, rpb,
                s5_lambda_re, s5_lambda_im, s5_log_dt, s5_b_re, s5_b_im, s5_c_re, s5_c_im, s5_d, glu_w, glu_b,
                branch_g, w_out, norm2_g, router_w, w_gate, w_up, w_down):
    h = rmsnorm(x, norm1_g)
    proj = h @ w_in
    o1 = LRU_WIDTH
    o2 = o1 + LRU_WIDTH
    o3 = o2 + NA_WIDTH
    o4 = o3 + NA_WIDTH
    o5 = o4 + NA_WIDTH
    ya = rglru_mixer(proj[..., :o1], proj[..., o1:o2], conv_w, conv_b, gate_a_w, gate_a_b, gate_x_w, gate_x_b, lru_lambda)
    yb = neighbourhood_attention(proj[..., o2:o3], proj[..., o3:o4], proj[..., o4:o5], rpb)
    yc = s5_mixer(proj[..., o5:], s5_lambda_re, s5_lambda_im, s5_log_dt, s5_b_re, s5_b_im, s5_c_re, s5_c_im, s5_d, glu_w, glu_b)
    mixed = jnp.concatenate([
        rmsnorm(ya, branch_g[:LRU_WIDTH]),
        rmsnorm(yb, branch_g[LRU_WIDTH:LRU_WIDTH + NA_WIDTH]),
        rmsnorm(yc, branch_g[LRU_WIDTH + NA_WIDTH:])], axis=-1)
    x = x + mixed @ w_out
    x = x + expert_choice_ffn(rmsnorm(x, norm2_g), router_w, w_gate, w_up, w_down)
    return x


def _final_norm_kernel(x_ref, g_ref, o_ref):
    x = x_ref[...]
    ms = jnp.mean(x * x, axis=-1, keepdims=True)
    o_ref[...] = x * lax.rsqrt(ms + EPS) * g_ref[...]


def final_rmsnorm(x2d, g):
    n, d = x2d.shape
    tm = 1024
    return pl.pallas_call(
        _final_norm_kernel,
        grid=(n // tm,),
        in_specs=[pl.BlockSpec((tm, d), lambda i: (i, 0)), pl.BlockSpec((1, d), lambda i: (0, 0))],
        out_specs=pl.BlockSpec((tm, d), lambda i: (i, 0)),
        out_shape=jax.ShapeDtypeStruct((n, d), jnp.float32),
        name="final_rmsnorm",
    )(x2d, g.reshape(1, d))


def kernel(x_prompt, x_sample, norm1_g, w_in, conv_w, conv_b, gate_a_w, gate_a_b, gate_x_w, gate_x_b,
           lru_lambda, rpb, s5_lambda_re, s5_lambda_im, s5_log_dt, s5_b_re, s5_b_im, s5_c_re, s5_c_im,
           s5_d, glu_w, glu_b, branch_g, w_out, norm2_g, router_w, w_gate, w_up, w_down, final_g):
    xp = x_prompt
    xs = x_sample
    for l in range(DEPTH):
        lp = (norm1_g[l], w_in[l], conv_w[l], conv_b[l], gate_a_w[l], gate_a_b[l], gate_x_w[l], gate_x_b[l],
              lru_lambda[l], rpb[l], s5_lambda_re[l], s5_lambda_im[l], s5_log_dt[l], s5_b_re[l], s5_b_im[l],
              s5_c_re[l], s5_c_im[l], s5_d[l], glu_w[l], glu_b[l], branch_g[l], w_out[l], norm2_g[l],
              router_w[l], w_gate[l], w_up[l], w_down[l])
        xp = trunk_layer(xp, *lp)
        xs = trunk_layer(xs, *lp)
    y_prompt = final_rmsnorm(xp.reshape(-1, D_MODEL), final_g).reshape(xp.shape)
    y_sample = final_rmsnorm(xs.reshape(-1, D_MODEL), final_g).reshape(xs.shape)
    return (y_prompt, y_sample)
```

```python
import functools

import jax
import jax.numpy as jnp
import numpy as np
from jax import lax
from jax.experimental import pallas as pl
from jax.experimental.pallas import tpu as pltpu

DEPTH = 4
LRU_WIDTH = 768
LRU_BLOCKS = 6
LRU_BLOCK = LRU_WIDTH // LRU_BLOCKS
CONV_W = 4
LRU_C = 8.0
NA_HEADS = 12
NA_HEAD_DIM = 64
NA_WIDTH = NA_HEADS * NA_HEAD_DIM
GRID_W = 64
WIN_ROWS = 8
WIN_COLS = 16
S5_WIDTH = 512
S5_GROUP = 16
S5_GROUPS = S5_WIDTH // S5_GROUP
S5_STATE = 64
MIX_WIDTH = LRU_WIDTH + NA_WIDTH + S5_WIDTH
IN_WIDTH = 2 * LRU_WIDTH + 3 * NA_WIDTH + S5_WIDTH
N_EXPERTS = 16
EC_CAPACITY = 2
EPS = 1e-6

SUBLANES = 8
LANES = 128
VMEM_LIMIT_BYTES = 52 * 1024 * 1024

S5_CHUNK = 32
S5_CW = S5_CHUNK * S5_GROUP
S5_SW = 4 * S5_STATE
NA_QROWS = 8
NA_BLOCK_TOKENS = NA_QROWS * GRID_W

F32 = jnp.float32
BF16 = jnp.bfloat16


def _cparams(*sem):
    return pltpu.CompilerParams(dimension_semantics=sem, vmem_limit_bytes=VMEM_LIMIT_BYTES)


def _rms(v, g):
    ms = jnp.mean(v * v, axis=-1, keepdims=True)
    return v * lax.rsqrt(ms + EPS) * g


def _in_proj_kernel(x_ref, g_ref, w_ref, o_ref):
    h = _rms(x_ref[...], g_ref[...]).astype(BF16)
    o_ref[...] = jnp.dot(h, w_ref[...], preferred_element_type=F32)


def in_proj(x2d, g, w_bf16, tm=512):
    n, d = x2d.shape
    nw = w_bf16.shape[1]
    tn = nw // 2
    assert n % tm == 0 and tn % LANES == 0
    return pl.pallas_call(
        _in_proj_kernel,
        grid=(2, n // tm),
        in_specs=[pl.BlockSpec((tm, d), lambda j, i: (i, 0)),
                  pl.BlockSpec((1, d), lambda j, i: (0, 0)),
                  pl.BlockSpec((d, tn), lambda j, i: (0, j))],
        out_specs=pl.BlockSpec((tm, tn), lambda j, i: (i, j)),
        out_shape=jax.ShapeDtypeStruct((n, nw), F32),
        compiler_params=_cparams("arbitrary", "arbitrary"),
        name="in_proj",
    )(x2d, g.reshape(1, d), w_bf16)


def _lru_kernel(rev, tl, nt, *refs):
    if rev:
        (xm_ref, xp_ref, xn_ref, cw_ref, cb_ref, wa_ref, wx_ref, ba_ref, bx_ref, c8_ref, ga_ref, hf_ref,
         o_ref, ext_ref, a_ref, b_ref, carry_ref) = refs
    else:
        (xm_ref, xp_ref, xn_ref, cw_ref, cb_ref, wa_ref, wx_ref, ba_ref, bx_ref, c8_ref,
         o_ref, ext_ref, a_ref, b_ref, carry_ref) = refs
    i = pl.program_id(1)
    ci = (nt - 1 - i) if rev else i
    w = LRU_WIDTH
    nb = tl // SUBLANES

    xm = xm_ref[0]
    ext_ref[SUBLANES:SUBLANES + tl, :] = xm
    ext_ref[0:SUBLANES, :] = jnp.where(ci > 0, xp_ref[0], 0.0)
    ext_ref[SUBLANES + tl:2 * SUBLANES + tl, :] = jnp.where(ci < nt - 1, xn_ref[0], 0.0)
    cw = cw_ref[...]
    xc = (cw[0:1] * ext_ref[SUBLANES - 2:SUBLANES - 2 + tl, :]
          + cw[1:2] * ext_ref[SUBLANES - 1:SUBLANES - 1 + tl, :]
          + cw[2:3] * xm
          + cw[3:4] * ext_ref[SUBLANES + 1:SUBLANES + 1 + tl, :]) + cb_ref[...]

    xcb = xc.astype(BF16)
    r_parts = []
    i_parts = []
    for h in range(LRU_BLOCKS):
        xb = xcb[:, h * LRU_BLOCK:(h + 1) * LRU_BLOCK]
        r_parts.append(jnp.dot(xb, wa_ref[h], preferred_element_type=F32))
        i_parts.append(jnp.dot(xb, wx_ref[h], preferred_element_type=F32))
    r = jax.nn.sigmoid(jnp.concatenate(r_parts, axis=1) + ba_ref[...])
    ig = jax.nn.sigmoid(jnp.concatenate(i_parts, axis=1) + bx_ref[...])
    a = jnp.exp(r * c8_ref[...])
    b = jnp.sqrt(1.0 - a * a) * (ig * xc)

    row = lax.broadcasted_iota(jnp.int32, (tl, w), 0) % SUBLANES
    for s in (1, 2, 4):
        if rev:
            m = row < SUBLANES - s
            a_sh = pltpu.roll(a, tl - s, 0)
            b_sh = pltpu.roll(b, tl - s, 0)
        else:
            m = row >= s
            a_sh = pltpu.roll(a, s, 0)
            b_sh = pltpu.roll(b, s, 0)
        b = b + a * jnp.where(m, b_sh, 0.0)
        a = a * jnp.where(m, a_sh, 1.0)
    a_ref[...] = a
    b_ref[...] = b

    @pl.when(i == 0)
    def _():
        carry_ref[...] = jnp.zeros_like(carry_ref)

    def body(j, c):
        jj = (nb - 1 - j) if rev else j
        st = pl.multiple_of(jj * SUBLANES, SUBLANES)
        h = b_ref[pl.ds(st, SUBLANES), :] + a_ref[pl.ds(st, SUBLANES), :] * c
        b_ref[pl.ds(st, SUBLANES), :] = h
        edge = h[0:1, :] if rev else h[SUBLANES - 1:SUBLANES, :]
        return jnp.broadcast_to(edge, (SUBLANES, w))

    carry_ref[...] = lax.fori_loop(0, nb, body, carry_ref[...])

    if rev:
        o_ref[0] = (hf_ref[0] + b_ref[...]) * jax.nn.gelu(ga_ref[0])
    else:
        o_ref[0] = b_ref[...]


def lru_scan(proj3, conv_w, conv_b, wa, wx, ba, bx, c8, rev, hf=None, tl=512):
    bn, L, _ = proj3.shape
    w = LRU_WIDTH
    assert L % tl == 0
    nt = L // tl
    tb = tl // SUBLANES
    nblk8 = L // SUBLANES

    def cidx(i):
        return (nt - 1 - i) if rev else i

    in_specs = [
        pl.BlockSpec((1, tl, w), lambda b, i: (b, cidx(i), 0)),
        pl.BlockSpec((1, SUBLANES, w), lambda b, i: (b, jnp.maximum(cidx(i) * tb - 1, 0), 0)),
        pl.BlockSpec((1, SUBLANES, w), lambda b, i: (b, jnp.minimum((cidx(i) + 1) * tb, nblk8 - 1), 0)),
        pl.BlockSpec((CONV_W, w), lambda b, i: (0, 0)),
        pl.BlockSpec((1, w), lambda b, i: (0, 0)),
        pl.BlockSpec((LRU_BLOCKS, LRU_BLOCK, LRU_BLOCK), lambda b, i: (0, 0, 0)),
        pl.BlockSpec((LRU_BLOCKS, LRU_BLOCK, LRU_BLOCK), lambda b, i: (0, 0, 0)),
        pl.BlockSpec((1, w), lambda b, i: (0, 0)),
        pl.BlockSpec((1, w), lambda b, i: (0, 0)),
        pl.BlockSpec((1, w), lambda b, i: (0, 0)),
    ]
    args = [proj3, proj3, proj3, conv_w, conv_b.reshape(1, w), wa, wx, ba.reshape(1, w), bx.reshape(1, w),
            c8.reshape(1, w)]
    if rev:
        in_specs += [pl.BlockSpec((1, tl, w), lambda b, i: (b, cidx(i), 1)),
                     pl.BlockSpec((1, tl, w), lambda b, i: (b, cidx(i), 0))]
        args += [proj3, hf]
    return pl.pallas_call(
        functools.partial(_lru_kernel, rev, tl, nt),
        grid=(bn, nt),
        in_specs=in_specs,
        out_specs=pl.BlockSpec((1, tl, w), lambda b, i: (b, cidx(i), 0)),
        out_shape=jax.ShapeDtypeStruct((bn, L, w), F32),
        scratch_shapes=[pltpu.VMEM((tl + 2 * SUBLANES, w), F32), pltpu.VMEM((tl, w), F32),
                        pltpu.VMEM((tl, w), F32), pltpu.VMEM((SUBLANES, w), F32)],
        compiler_params=_cparams("arbitrary", "arbitrary"),
        name="lru_scan_bwd" if rev else "lru_scan_fwd",
    )(*args)


def na_bias_patterns(rpb):
    qc = np.arange(GRID_W)[:, None]
    kc = np.arange(GRID_W)[None, :]
    cs = np.clip(qc - WIN_COLS // 2, 0, GRID_W - WIN_COLS)
    mask = (kc >= cs) & (kc < cs + WIN_COLS)
    dc = np.clip(kc - qc + WIN_COLS - 1, 0, 2 * WIN_COLS - 2)
    bias2d = jnp.where(jnp.asarray(mask)[None, None], rpb.astype(F32)[:, :, dc], -1e30)
    dr = np.arange(WIN_ROWS)[None, :] - np.arange(WIN_ROWS)[:, None] + WIN_ROWS - 1
    pat = bias2d[:, dr]
    return pat.transpose(0, 1, 3, 2, 4).reshape(NA_HEADS, WIN_ROWS, GRID_W, WIN_ROWS * GRID_W)


def _na_kernel(rows, q_ref, kp_ref, kc_ref, kn_ref, vp_ref, vc_ref, vn_ref, bias_ref, o_ref, kext_ref, vext_ref):
    rb = pl.program_id(2)
    nbt = NA_BLOCK_TOKENS
    kext_ref[0:nbt, :] = kp_ref[0].astype(BF16)
    kext_ref[nbt:2 * nbt, :] = kc_ref[0].astype(BF16)
    kext_ref[2 * nbt:3 * nbt, :] = kn_ref[0].astype(BF16)
    vext_ref[0:nbt, :] = vp_ref[0].astype(BF16)
    vext_ref[nbt:2 * nbt, :] = vc_ref[0].astype(BF16)
    vext_ref[2 * nbt:3 * nbt, :] = vn_ref[0].astype(BF16)
    lane = lax.broadcasted_iota(jnp.int32, (GRID_W, LANES), 1)
    win = WIN_ROWS * GRID_W

    def body(rl, carry):
        r = rb * NA_QROWS + rl
        rs = jnp.clip(r - WIN_ROWS // 2, 0, rows - WIN_ROWS)
        p = r - rs
        off = pl.multiple_of((rs - rb * NA_QROWS + NA_QROWS) * GRID_W, GRID_W)
        qoff = pl.multiple_of(rl * GRID_W, GRID_W)
        q = q_ref[0, pl.ds(qoff, GRID_W), :] * (NA_HEAD_DIM ** -0.5)
        kw = kext_ref[pl.ds(off, win), :]
        vw = vext_ref[pl.ds(off, win), :]
        outs = []
        for hh in range(2):
            sel = (lane >= NA_HEAD_DIM * hh) & (lane < NA_HEAD_DIM * (hh + 1))
            qh = jnp.where(sel, q, 0.0).astype(BF16)
            s = lax.dot_general(qh, kw, (((1,), (1,)), ((), ())), preferred_element_type=F32)
            s = s + bias_ref[hh, p]
            m = jnp.max(s, axis=-1, keepdims=True)
            e = jnp.exp(s - m)
            l = jnp.sum(e, axis=-1, keepdims=True)
            outs.append(jnp.dot(e.astype(BF16), vw, preferred_element_type=F32) / l)
        o_ref[0, pl.ds(qoff, GRID_W), :] = jnp.where(lane < NA_HEAD_DIM, outs[0], outs[1])
        return carry

    lax.fori_loop(0, NA_QROWS, body, 0)


def na_attention(proj3, bias_pat):
    bn, L, _ = proj3.shape
    rows = L // GRID_W
    assert rows % NA_QROWS == 0 and rows >= WIN_ROWS
    nrb = rows // NA_QROWS
    qb = 2 * LRU_WIDTH // LANES
    kb = qb + NA_WIDTH // LANES
    vb = kb + NA_WIDTH // LANES
    nbt = NA_BLOCK_TOKENS

    def blk(col0, shift):
        return pl.BlockSpec((1, nbt, LANES),
                            lambda b, hp, rb: (b, jnp.clip(rb + shift, 0, nrb - 1), col0 + hp))

    return pl.pallas_call(
        functools.partial(_na_kernel, rows),
        grid=(bn, NA_HEADS // 2, nrb),
        in_specs=[blk(qb, 0), blk(kb, -1), blk(kb, 0), blk(kb, 1), blk(vb, -1), blk(vb, 0), blk(vb, 1),
                  pl.BlockSpec((2, WIN_ROWS, GRID_W, WIN_ROWS * GRID_W), lambda b, hp, rb: (hp, 0, 0, 0))],
        out_specs=pl.BlockSpec((1, nbt, LANES), lambda b, hp, rb: (b, rb, hp)),
        out_shape=jax.ShapeDtypeStruct((bn, L, NA_WIDTH), F32),
        scratch_shapes=[pltpu.VMEM((3 * nbt, LANES), BF16), pltpu.VMEM((3 * nbt, LANES), BF16)],
        compiler_params=_cparams("arbitrary", "arbitrary", "arbitrary"),
        name="na_attention",
    )(proj3, proj3, proj3, proj3, proj3, proj3, proj3, bias_pat)


def s5_prep(lam_re, lam_im, log_dt, b_re, b_im, c_re, c_im, levels):
    T = S5_CHUNK
    hp = lax.Precision.HIGHEST
    ks = np.arange(T + 1)
    per_dir = []
    for d in range(2):
        lr = jnp.minimum(lam_re[d].astype(F32), -1e-4)
        li = lam_im[d].astype(F32)
        dt = jnp.exp(log_dt[d].astype(F32))[:, None]
        mag = jnp.exp(lr * dt)
        ar = mag * jnp.cos(li * dt)
        ai = mag * jnp.sin(li * dt)
        den = lr * lr + li * li
        nr = ar - 1.0
        fr = (nr * lr + ai * li) / den
        fi = (ai * lr - nr * li) / den
        br_d = b_re[d].astype(F32)
        bi_d = b_im[d].astype(F32)
        bbr = fr[..., None] * br_d - fi[..., None] * bi_d
        bbi = fr[..., None] * bi_d + fi[..., None] * br_d
        pr = jnp.ones((T + 1,) + ar.shape, F32)
        pi = jnp.zeros((T + 1,) + ar.shape, F32)
        cr, ci = ar, ai
        for bit in range(int(T).bit_length()):
            sel = jnp.asarray(((ks >> bit) & 1) == 1)[:, None, None]
            pr, pi = jnp.where(sel, pr * cr - pi * ci, pr), jnp.where(sel, pr * ci + pi * cr, pi)
            cr, ci = cr * cr - ci * ci, 2.0 * cr * ci
        per_dir.append((pr, pi, bbr, bbi, c_re[d].astype(F32), c_im[d].astype(F32)))

    def lag_kernels(pr, pi, bbr, bbi, cre, cim):
        car = cre[None] * pr[:T, :, None, :] - cim[None] * pi[:T, :, None, :]
        cai = cre[None] * pi[:T, :, None, :] + cim[None] * pr[:T, :, None, :]
        return (jnp.einsum('kgop,gpc->kgoc', car, bbr, precision=hp)
                - jnp.einsum('kgop,gpc->kgoc', cai, bbi, precision=hp))

    kf = lag_kernels(*per_dir[0])
    kb = lag_kernels(*per_dir[1])
    kcat = jnp.concatenate([kb[:0:-1], (kf[0] + kb[0])[None], kf[1:]], axis=0)
    didx = np.arange(T)[None, :] - np.arange(T)[:, None] + T - 1
    m = kcat[didx].transpose(2, 0, 4, 1, 3).reshape(S5_GROUPS, S5_CW, S5_CW)

    prf, pif, bbrf, bbif, cref, cimf = per_dir[0]
    prb, pib, bbrb, bbib, creb, cimb = per_dir[1]
    jf = T - 1 - np.arange(T)
    jb = np.arange(T)

    def inj(pr, pi, bbr, bbi, e):
        re = pr[e][..., None] * bbr[None] - pi[e][..., None] * bbi[None]
        im = pr[e][..., None] * bbi[None] + pi[e][..., None] * bbr[None]
        return re, im

    sfr, sfi = inj(prf, pif, bbrf, bbif, jf)
    sbr, sbi = inj(prb, pib, bbrb, bbib, jb)
    p = jnp.concatenate([sfr, sfi, sbr, sbi], axis=2).transpose(1, 0, 3, 2).reshape(S5_GROUPS, S5_CW, S5_SW)

    def readout(pr, pi, cre, cim, e):
        re = cre[None] * pr[e][:, :, None, :] - cim[None] * pi[e][:, :, None, :]
        im = cre[None] * pi[e][:, :, None, :] + cim[None] * pr[e][:, :, None, :]
        return re, -im

    qfr, qfi = readout(prf, pif, cref, cimf, 1 + np.arange(T))
    qbr, qbi = readout(prb, pib, creb, cimb, T - np.arange(T))
    q = jnp.concatenate([qfr, qfi, qbr, qbi], axis=3).transpose(1, 3, 0, 2).reshape(S5_GROUPS, S5_SW, S5_CW)

    rows = []
    fr_, fi_ = prf[T], pif[T]
    br_, bi_ = prb[T], pib[T]
    for _ in range(levels):
        rows.append(jnp.concatenate([fr_, fr_, br_, br_], axis=1))
        rows.append(jnp.concatenate([-fi_, fi_, -bi_, bi_], axis=1))
        fr_, fi_ = fr_ * fr_ - fi_ * fi_, 2.0 * fr_ * fi_
        br_, bi_ = br_ * br_ - bi_ * bi_, 2.0 * br_ * bi_
    dec = jnp.stack(rows, axis=1)
    return m.astype(BF16), p.astype(BF16), q.astype(BF16), dec


def _s5_kernel(cps, levels, u_ref, m_ref, p_ref, q_ref, dec_ref, y_ref):
    u = u_ref[...]
    s = jnp.dot(u, p_ref[0], preferred_element_type=F32)
    rb = s.shape[0]
    pos = lax.broadcasted_iota(jnp.int32, (rb, LANES), 0) % cps
    sf = s[:, :LANES]
    sb = s[:, LANES:]
    half = S5_STATE
    for k in range(levels):
        sh = 1 << k
        a1 = dec_ref[0, 2 * k:2 * k + 1, :]
        a2 = dec_ref[0, 2 * k + 1:2 * k + 2, :]
        xs = jnp.where(pos >= sh, pltpu.roll(sf, sh, 0), 0.0)
        sf = sf + a1[:, :LANES] * xs + a2[:, :LANES] * pltpu.roll(xs, half, 1)
        xs = jnp.where(pos < cps - sh, pltpu.roll(sb, rb - sh, 0), 0.0)
        sb = sb + a1[:, LANES:] * xs + a2[:, LANES:] * pltpu.roll(xs, half, 1)
    pf = jnp.where(pos >= 1, pltpu.roll(sf, 1, 0), 0.0)
    pb = jnp.where(pos < cps - 1, pltpu.roll(sb, rb - 1, 0), 0.0)
    sprev = jnp.concatenate([pf, pb], axis=1).astype(BF16)
    y_ref[...] = (jnp.dot(u, m_ref[0], preferred_element_type=F32)
                  + jnp.dot(sprev, q_ref[0], preferred_element_type=F32))


def s5_chunk_scan(uc, m, p, q, dec, cps, levels):
    nchunks = uc.shape[0]
    rb = min(nchunks, 1024)
    assert nchunks % rb == 0 and rb % cps == 0 and (1 << levels) >= cps
    return pl.pallas_call(
        functools.partial(_s5_kernel, cps, levels),
        grid=(nchunks // rb, S5_GROUPS),
        in_specs=[pl.BlockSpec((rb, S5_CW), lambda i, g: (i, g)),
                  pl.BlockSpec((1, S5_CW, S5_CW), lambda i, g: (g, 0, 0)),
                  pl.BlockSpec((1, S5_CW, S5_SW), lambda i, g: (g, 0, 0)),
                  pl.BlockSpec((1, S5_SW, S5_CW), lambda i, g: (g, 0, 0)),
                  pl.BlockSpec((1, dec.shape[1], S5_SW), lambda i, g: (g, 0, 0))],
        out_specs=pl.BlockSpec((rb, S5_CW), lambda i, g: (i, g)),
        out_shape=jax.ShapeDtypeStruct((nchunks, S5_GROUPS * S5_CW), F32),
        compiler_params=_cparams("arbitrary", "arbitrary"),
        name="s5_chunk_scan",
    )(uc, m, p, q, dec)


def _s5_post_kernel(y_ref, u_ref, d_ref, w_ref, b_ref, o_ref):
    y = y_ref[...] + u_ref[...] * d_ref[...]
    z = jax.nn.gelu(y).astype(BF16)
    zz = jnp.dot(z, w_ref[...], preferred_element_type=F32) + b_ref[...]
    o_ref[...] = zz[:, :S5_WIDTH] * jax.nn.sigmoid(zz[:, S5_WIDTH:])


def s5_post(y, u, d_skip, glu_w_bf16, glu_b, tm=1024):
    n = y.shape[0]
    w = S5_WIDTH
    assert n % tm == 0
    return pl.pallas_call(
        _s5_post_kernel,
        grid=(n // tm,),
        in_specs=[pl.BlockSpec((tm, w), lambda i: (i, 0)), pl.BlockSpec((tm, w), lambda i: (i, 0)),
                  pl.BlockSpec((1, w), lambda i: (0, 0)), pl.BlockSpec((w, 2 * w), lambda i: (0, 0)),
                  pl.BlockSpec((1, 2 * w), lambda i: (0, 0))],
        out_specs=pl.BlockSpec((tm, w), lambda i: (i, 0)),
        out_shape=jax.ShapeDtypeStruct((n, w), F32),
        compiler_params=_cparams("arbitrary"),
        name="s5_post",
    )(y, u, d_skip.reshape(1, w), glu_w_bf16, glu_b.reshape(1, 2 * w))


def s5_mixer(u, s5w, d_skip, glu_w_bf16, glu_b, seq_len):
    n = u.shape[0]
    T = S5_CHUNK
    m, p, q, dec = s5w
    cps = seq_len // T
    levels = int(cps).bit_length() - 1
    assert (1 << levels) == cps and seq_len % T == 0
    uc = u.reshape(n // T, T, S5_GROUPS, S5_GROUP).transpose(0, 2, 1, 3).reshape(n // T, S5_GROUPS * S5_CW)
    yc = s5_chunk_scan(uc.astype(BF16), m, p, q, dec, cps, levels)
    y = yc.reshape(n // T, S5_GROUPS, T, S5_GROUP).transpose(0, 2, 1, 3).reshape(n, S5_WIDTH)
    return s5_post(y, u, d_skip, glu_w_bf16, glu_b)


def _out_proj_kernel(ya_ref, yb_ref, yc_ref, x_ref, bg_ref, w_ref, g2_ref, rwh_ref, rwl_ref,
                     x1_ref, hn_ref, aff_ref):
    o1 = LRU_WIDTH
    o2 = LRU_WIDTH + NA_WIDTH
    ma = _rms(ya_ref[...], bg_ref[:, 0:o1]).astype(BF16)
    mb = _rms(yb_ref[...], bg_ref[:, o1:o2]).astype(BF16)
    mc = _rms(yc_ref[...], bg_ref[:, o2:MIX_WIDTH]).astype(BF16)
    x1 = (x_ref[...]
          + jnp.dot(ma, w_ref[0:o1, :], preferred_element_type=F32)
          + jnp.dot(mb, w_ref[o1:o2, :], preferred_element_type=F32)
          + jnp.dot(mc, w_ref[o2:MIX_WIDTH, :], preferred_element_type=F32))
    x1_ref[...] = x1
    h2 = _rms(x1, g2_ref[...])
    hb = h2.astype(BF16)
    hn_ref[...] = hb
    lo = (h2 - hb.astype(F32)).astype(BF16)
    logits = (jnp.dot(hb, rwh_ref[...], preferred_element_type=F32)
              + jnp.dot(hb, rwl_ref[...], preferred_element_type=F32)
              + jnp.dot(lo, rwh_ref[...], preferred_element_type=F32))
    lane = lax.broadcasted_iota(jnp.int32, logits.shape, 1)
    logits = jnp.where(lane < N_EXPERTS, logits, -1e30)
    m = jnp.max(logits, axis=-1, keepdims=True)
    e = jnp.exp(logits - m)
    aff_ref[...] = e / jnp.sum(e, axis=-1, keepdims=True)


def out_proj(ya, yb, yc, x2d, branch_g, w_out_bf16, norm2_g, rw_hi, rw_lo, tm=256):
    n, d = x2d.shape
    assert n % tm == 0
    row = lambda i: (i, 0)
    fixed = lambda i: (0, 0)
    return pl.pallas_call(
        _out_proj_kernel,
        grid=(n // tm,),
        in_specs=[pl.BlockSpec((tm, LRU_WIDTH), row), pl.BlockSpec((tm, NA_WIDTH), row),
                  pl.BlockSpec((tm, S5_WIDTH), row), pl.BlockSpec((tm, d), row),
                  pl.BlockSpec((1, MIX_WIDTH), fixed), pl.BlockSpec((MIX_WIDTH, d), fixed),
                  pl.BlockSpec((1, d), fixed), pl.BlockSpec((d, LANES), fixed), pl.BlockSpec((d, LANES), fixed)],
        out_specs=[pl.BlockSpec((tm, d), row), pl.BlockSpec((tm, d), row), pl.BlockSpec((tm, LANES), row)],
        out_shape=[jax.ShapeDtypeStruct((n, d), F32), jax.ShapeDtypeStruct((n, d), BF16),
                   jax.ShapeDtypeStruct((n, LANES), F32)],
        compiler_params=_cparams("arbitrary"),
        name="out_proj_router",
    )(ya, yb, yc, x2d, branch_g.reshape(1, MIX_WIDTH), w_out_bf16, norm2_g.reshape(1, d), rw_hi, rw_lo)


def _ffn_kernel(nf, x_ref, wg_ref, wu_ref, wd_ref, g_ref, o_ref):
    f = pl.program_id(2)
    x = x_ref[0]
    gate = jnp.dot(x, wg_ref[0], preferred_element_type=F32)
    up = jnp.dot(x, wu_ref[0], preferred_element_type=F32)
    h = (gate * jax.nn.sigmoid(gate) * up).astype(BF16)
    contrib = jnp.dot(h, wd_ref[0], preferred_element_type=F32)

    @pl.when(f == 0)
    def _():
        o_ref[0] = contrib

    @pl.when(f > 0)
    def _():
        o_ref[0] += contrib

    @pl.when(f == nf - 1)
    def _():
        o_ref[0] = o_ref[0] * g_ref[0]


def expert_ffn(xe, wg, wu, wd, g3, tm=1024, tf=512):
    ne, cap, d = xe.shape
    ff = wg.shape[2]
    tm = min(tm, cap)
    assert cap % tm == 0 and ff % tf == 0
    nf = ff // tf
    return pl.pallas_call(
        functools.partial(_ffn_kernel, nf),
        grid=(ne, cap // tm, nf),
        in_specs=[pl.BlockSpec((1, tm, d), lambda e, i, f: (e, i, 0)),
                  pl.BlockSpec((1, d, tf), lambda e, i, f: (e, 0, f)),
                  pl.BlockSpec((1, d, tf), lambda e, i, f: (e, 0, f)),
                  pl.BlockSpec((1, tf, d), lambda e, i, f: (e, f, 0)),
                  pl.BlockSpec((1, tm, 1), lambda e, i, f: (e, i, 0))],
        out_specs=pl.BlockSpec((1, tm, d), lambda e, i, f: (e, i, 0)),
        out_shape=jax.ShapeDtypeStruct((ne, cap, d), F32),
        compiler_params=_cparams("arbitrary", "arbitrary", "arbitrary"),
        name="expert_ffn",
    )(xe, wg, wu, wd, g3)


def _final_norm_kernel(x_ref, g_ref, o_ref):
    o_ref[...] = _rms(x_ref[...], g_ref[...])


def final_rmsnorm(x2d, g, tm=1024):
    n, d = x2d.shape
    tm = min(tm, n)
    assert n % tm == 0
    return pl.pallas_call(
        _final_norm_kernel,
        grid=(n // tm,),
        in_specs=[pl.BlockSpec((tm, d), lambda i: (i, 0)), pl.BlockSpec((1, d), lambda i: (0, 0))],
        out_specs=pl.BlockSpec((tm, d), lambda i: (i, 0)),
        out_shape=jax.ShapeDtypeStruct((n, d), F32),
        compiler_params=_cparams("arbitrary"),
        name="final_rmsnorm",
    )(x2d, g.reshape(1, d))


def prep_layer(norm1_g, w_in, conv_w, conv_b, gate_a_w, gate_a_b, gate_x_w, gate_x_b, lru_lambda, rpb,
               s5_lambda_re, s5_lambda_im, s5_log_dt, s5_b_re, s5_b_im, s5_c_re, s5_c_im, s5_d, glu_w, glu_b,
               branch_g, w_out, norm2_g, router_w, w_gate, w_up, w_down, s5_levels):
    d = router_w.shape[0]
    rw = jnp.zeros((d, LANES), F32).at[:, :N_EXPERTS].set(router_w.astype(F32))
    rw_hi = rw.astype(BF16)
    rw_lo = (rw - rw_hi.astype(F32)).astype(BF16)
    return dict(
        norm1_g=norm1_g, w_in=w_in.astype(BF16), conv_w=conv_w.astype(F32), conv_b=conv_b.astype(F32),
        wa=gate_a_w.astype(BF16), wx=gate_x_w.astype(BF16), ba=gate_a_b.astype(F32), bx=gate_x_b.astype(F32),
        c8=-LRU_C * jax.nn.softplus(-lru_lambda.astype(F32)),
        bias_pat=na_bias_patterns(rpb),
        s5w=s5_prep(s5_lambda_re, s5_lambda_im, s5_log_dt, s5_b_re, s5_b_im, s5_c_re, s5_c_im, s5_levels),
        s5_d=s5_d.astype(F32), glu_w=glu_w.astype(BF16), glu_b=glu_b.astype(F32),
        branch_g=branch_g.astype(F32), w_out=w_out.astype(BF16), norm2_g=norm2_g.astype(F32),
        rw_hi=rw_hi, rw_lo=rw_lo,
        w_gate=w_gate.astype(BF16), w_up=w_up.astype(BF16), w_down=w_down.astype(BF16))


def trunk_layer(x, lw):
    bn, L, d = x.shape
    n = bn * L
    x2d = x.reshape(n, d)
    proj = in_proj(x2d, lw['norm1_g'], lw['w_in'])
    proj3 = proj.reshape(bn, L, IN_WIDTH)
    hf = lru_scan(proj3, lw['conv_w'], lw['conv_b'], lw['wa'][0], lw['wx'][0], lw['ba'][0], lw['bx'][0],
                  lw['c8'][0], rev=False)
    ya = lru_scan(proj3, lw['conv_w'], lw['conv_b'], lw['wa'][1], lw['wx'][1], lw['ba'][1], lw['bx'][1],
                  lw['c8'][1], rev=True, hf=hf)
    yb = na_attention(proj3, lw['bias_pat'])
    u = proj[:, IN_WIDTH - S5_WIDTH:]
    yc = s5_mixer(u, lw['s5w'], lw['s5_d'], lw['glu_w'], lw['glu_b'], L)
    x1, hn, aff = out_proj(ya.reshape(n, LRU_WIDTH), yb.reshape(n, NA_WIDTH), yc, x2d, lw['branch_g'],
                           lw['w_out'], lw['norm2_g'], lw['rw_hi'], lw['rw_lo'])
    cap = max(1, EC_CAPACITY * n // N_EXPERTS)
    g, idx = lax.top_k(aff[:, :N_EXPERTS].T, cap)
    xe = hn[idx]
    ye = expert_ffn(xe, lw['w_gate'], lw['w_up'], lw['w_down'], g[..., None])
    x2 = x1.at[idx.reshape(-1)].add(ye.reshape(-1, d))
    return x2.reshape(bn, L, d)


def kernel(x_prompt, x_sample, norm1_g, w_in, conv_w, conv_b, gate_a_w, gate_a_b, gate_x_w, gate_x_b,
           lru_lambda, rpb, s5_lambda_re, s5_lambda_im, s5_log_dt, s5_b_re, s5_b_im, s5_c_re, s5_c_im,
           s5_d, glu_w, glu_b, branch_g, w_out, norm2_g, router_w, w_gate, w_up, w_down, final_g):
    xp = x_prompt
    xs = x_sample
    max_len = max(xp.shape[1], xs.shape[1])
    s5_levels = int(max_len // S5_CHUNK).bit_length() - 1
    for l in range(DEPTH):
        lw = prep_layer(norm1_g[l], w_in[l], conv_w[l], conv_b[l], gate_a_w[l], gate_a_b[l], gate_x_w[l],
                        gate_x_b[l], lru_lambda[l], rpb[l], s5_lambda_re[l], s5_lambda_im[l], s5_log_dt[l],
                        s5_b_re[l], s5_b_im[l], s5_c_re[l], s5_c_im[l], s5_d[l], glu_w[l], glu_b[l],
                        branch_g[l], w_out[l], norm2_g[l], router_w[l], w_gate[l], w_up[l], w_down[l], s5_levels)
        xp = trunk_layer(xp, lw)
        xs = trunk_layer(xs, lw)
    d = xp.shape[-1]
    y_prompt = final_rmsnorm(xp.reshape(-1, d), final_g).reshape(xp.shape)
    y_sample = final_rmsnorm(xs.reshape(-1, d), final_g).reshape(xs.shape)
    return (y_prompt, y_sample)
```

```python
import functools

import jax
import jax.numpy as jnp
import numpy as np
from jax import lax
from jax.experimental import pallas as pl
from jax.experimental.pallas import tpu as pltpu

DEPTH = 4
LRU_WIDTH = 768
LRU_BLOCKS = 6
LRU_BLOCK = LRU_WIDTH // LRU_BLOCKS
CONV_W = 4
LRU_C = 8.0
NA_HEADS = 12
NA_HEAD_DIM = 64
NA_WIDTH = NA_HEADS * NA_HEAD_DIM
GRID_W = 64
WIN_ROWS = 8
WIN_COLS = 16
S5_WIDTH = 512
S5_GROUP = 16
S5_GROUPS = S5_WIDTH // S5_GROUP
S5_STATE = 64
MIX_WIDTH = LRU_WIDTH + NA_WIDTH + S5_WIDTH
IN_WIDTH = 2 * LRU_WIDTH + 3 * NA_WIDTH + S5_WIDTH
N_EXPERTS = 16
EC_CAPACITY = 2
EPS = 1e-6

SUBLANES = 8
LANES = 128
VMEM_LIMIT_BYTES = 52 * 1024 * 1024

S5_CHUNK = 32
S5_CW = S5_CHUNK * S5_GROUP
S5_SW = 4 * S5_STATE
NA_QROWS = 8
NA_BLOCK_TOKENS = NA_QROWS * GRID_W
NA_KROWS = NA_QROWS + WIN_ROWS

F32 = jnp.float32
BF16 = jnp.bfloat16


def _cparams(*sem):
    return pltpu.CompilerParams(dimension_semantics=sem, vmem_limit_bytes=VMEM_LIMIT_BYTES)


def _rms(v, g):
    ms = jnp.mean(v * v, axis=-1, keepdims=True)
    return v * lax.rsqrt(ms + EPS) * g


def _in_proj_kernel(x_ref, g_ref, w_ref, o_ref):
    h = _rms(x_ref[...], g_ref[...]).astype(BF16)
    o_ref[...] = jnp.dot(h, w_ref[0], preferred_element_type=F32)


def in_proj(x2d, g, w_bf16, layer, tm=512):
    n, d = x2d.shape
    nw = w_bf16.shape[2]
    tn = nw // 2
    assert n % tm == 0 and tn % LANES == 0
    return pl.pallas_call(
        _in_proj_kernel,
        grid=(2, n // tm),
        in_specs=[pl.BlockSpec((tm, d), lambda j, i: (i, 0)),
                  pl.BlockSpec((1, d), lambda j, i: (0, 0)),
                  pl.BlockSpec((1, d, tn), lambda j, i: (layer, 0, j))],
        out_specs=pl.BlockSpec((tm, tn), lambda j, i: (i, j)),
        out_shape=jax.ShapeDtypeStruct((n, nw), F32),
        compiler_params=_cparams("arbitrary", "arbitrary"),
        name="in_proj",
    )(x2d, g.reshape(1, d), w_bf16)


def _lru_kernel(rev, tl, nt, *refs):
    if rev:
        (xm_ref, xp_ref, xn_ref, cw_ref, cb_ref, wa_ref, wx_ref, ba_ref, bx_ref, c8_ref, ga_ref, hf_ref,
         o_ref, ext_ref, a_ref, b_ref, carry_ref) = refs
    else:
        (xm_ref, xp_ref, xn_ref, cw_ref, cb_ref, wa_ref, wx_ref, ba_ref, bx_ref, c8_ref,
         o_ref, ext_ref, a_ref, b_ref, carry_ref) = refs
    i = pl.program_id(1)
    ci = (nt - 1 - i) if rev else i
    w = LRU_WIDTH
    nb = tl // SUBLANES

    xm = xm_ref[0]
    ext_ref[SUBLANES:SUBLANES + tl, :] = xm
    ext_ref[0:SUBLANES, :] = jnp.where(ci > 0, xp_ref[0], 0.0)
    ext_ref[SUBLANES + tl:2 * SUBLANES + tl, :] = jnp.where(ci < nt - 1, xn_ref[0], 0.0)
    cw = cw_ref[...]
    xc = (cw[0:1] * ext_ref[SUBLANES - 2:SUBLANES - 2 + tl, :]
          + cw[1:2] * ext_ref[SUBLANES - 1:SUBLANES - 1 + tl, :]
          + cw[2:3] * xm
          + cw[3:4] * ext_ref[SUBLANES + 1:SUBLANES + 1 + tl, :]) + cb_ref[...]

    xcb = xc.astype(BF16)
    r_parts = []
    i_parts = []
    for h in range(LRU_BLOCKS):
        xb = xcb[:, h * LRU_BLOCK:(h + 1) * LRU_BLOCK]
        r_parts.append(jnp.dot(xb, wa_ref[h], preferred_element_type=F32))
        i_parts.append(jnp.dot(xb, wx_ref[h], preferred_element_type=F32))
    r = jax.nn.sigmoid(jnp.concatenate(r_parts, axis=1) + ba_ref[...])
    ig = jax.nn.sigmoid(jnp.concatenate(i_parts, axis=1) + bx_ref[...])
    a = jnp.exp(r * c8_ref[...])
    b = jnp.sqrt(1.0 - a * a) * (ig * xc)

    row = lax.broadcasted_iota(jnp.int32, (tl, w), 0) % SUBLANES
    for s in (1, 2, 4):
        if rev:
            m = row < SUBLANES - s
            a_sh = pltpu.roll(a, tl - s, 0)
            b_sh = pltpu.roll(b, tl - s, 0)
        else:
            m = row >= s
            a_sh = pltpu.roll(a, s, 0)
            b_sh = pltpu.roll(b, s, 0)
        b = b + a * jnp.where(m, b_sh, 0.0)
        a = a * jnp.where(m, a_sh, 1.0)
    a_ref[...] = a
    b_ref[...] = b

    @pl.when(i == 0)
    def _():
        carry_ref[...] = jnp.zeros_like(carry_ref)

    def body(j, c):
        jj = (nb - 1 - j) if rev else j
        st = pl.multiple_of(jj * SUBLANES, SUBLANES)
        h = b_ref[pl.ds(st, SUBLANES), :] + a_ref[pl.ds(st, SUBLANES), :] * c
        b_ref[pl.ds(st, SUBLANES), :] = h
        edge = h[0:1, :] if rev else h[SUBLANES - 1:SUBLANES, :]
        return jnp.broadcast_to(edge, (SUBLANES, w))

    carry_ref[...] = lax.fori_loop(0, nb, body, carry_ref[...])

    if rev:
        o_ref[0] = (hf_ref[0] + b_ref[...]) * jax.nn.gelu(ga_ref[0])
    else:
        o_ref[0] = b_ref[...]


def lru_scan(proj3, conv_w, conv_b, wa, wx, ba, bx, c8, rev, hf=None, tl=512):
    bn, L, _ = proj3.shape
    w = LRU_WIDTH
    assert L % tl == 0
    nt = L // tl
    tb = tl // SUBLANES
    nblk8 = L // SUBLANES

    def cidx(i):
        return (nt - 1 - i) if rev else i

    in_specs = [
        pl.BlockSpec((1, tl, w), lambda b, i: (b, cidx(i), 0)),
        pl.BlockSpec((1, SUBLANES, w), lambda b, i: (b, jnp.maximum(cidx(i) * tb - 1, 0), 0)),
        pl.BlockSpec((1, SUBLANES, w), lambda b, i: (b, jnp.minimum((cidx(i) + 1) * tb, nblk8 - 1), 0)),
        pl.BlockSpec((CONV_W, w), lambda b, i: (0, 0)),
        pl.BlockSpec((1, w), lambda b, i: (0, 0)),
        pl.BlockSpec((LRU_BLOCKS, LRU_BLOCK, LRU_BLOCK), lambda b, i: (0, 0, 0)),
        pl.BlockSpec((LRU_BLOCKS, LRU_BLOCK, LRU_BLOCK), lambda b, i: (0, 0, 0)),
        pl.BlockSpec((1, w), lambda b, i: (0, 0)),
        pl.BlockSpec((1, w), lambda b, i: (0, 0)),
        pl.BlockSpec((1, w), lambda b, i: (0, 0)),
    ]
    args = [proj3, proj3, proj3, conv_w, conv_b.reshape(1, w), wa, wx, ba.reshape(1, w), bx.reshape(1, w),
            c8.reshape(1, w)]
    if rev:
        in_specs += [pl.BlockSpec((1, tl, w), lambda b, i: (b, cidx(i), 1)),
                     pl.BlockSpec((1, tl, w), lambda b, i: (b, cidx(i), 0))]
        args += [proj3, hf]
    return pl.pallas_call(
        functools.partial(_lru_kernel, rev, tl, nt),
        grid=(bn, nt),
        in_specs=in_specs,
        out_specs=pl.BlockSpec((1, tl, w), lambda b, i: (b, cidx(i), 0)),
        out_shape=jax.ShapeDtypeStruct((bn, L, w), F32),
        scratch_shapes=[pltpu.VMEM((tl + 2 * SUBLANES, w), F32), pltpu.VMEM((tl, w), F32),
                        pltpu.VMEM((tl, w), F32), pltpu.VMEM((SUBLANES, w), F32)],
        compiler_params=_cparams("arbitrary", "arbitrary"),
        name="lru_scan_bwd" if rev else "lru_scan_fwd",
    )(*args)


def na_bias_patterns(rpb):
    half = WIN_ROWS // 2
    qc = np.arange(GRID_W)[:, None]
    kc = np.arange(GRID_W)[None, :]
    cs = np.clip(qc - WIN_COLS // 2, 0, GRID_W - WIN_COLS)
    colmask = (kc >= cs) & (kc < cs + WIN_COLS)
    dc = np.clip(kc - qc + WIN_COLS - 1, 0, 2 * WIN_COLS - 2)
    rl = np.arange(NA_QROWS)
    kr = np.arange(NA_KROWS)
    w0 = np.stack([np.maximum(rl - half, 0), rl - half, np.minimum(rl - half, 0)])
    rel = kr[None, None, :] - half - w0[:, :, None]
    valid = (rel >= 0) & (rel < WIN_ROWS)
    dr = np.clip(kr[None, :] - rl[:, None] + WIN_ROWS - 1 - half, 0, 2 * WIN_ROWS - 2)
    vals = rpb.astype(F32)[:, dr][:, :, :, dc]
    mask = valid[:, :, :, None, None] & colmask[None, None, None]
    full = jnp.where(jnp.asarray(mask)[None], vals[:, None], -1e30)
    return full.transpose(0, 1, 2, 4, 3, 5).reshape(NA_HEADS, 3, NA_BLOCK_TOKENS, NA_KROWS * GRID_W)


def _na_kernel(q_ref, kp_ref, kc_ref, kn_ref, vp_ref, vc_ref, vn_ref, bias_ref, o_ref):
    kw = jnp.concatenate([kp_ref[0].astype(BF16), kc_ref[0].astype(BF16), kn_ref[0].astype(BF16)], axis=0)
    vw = jnp.concatenate([vp_ref[0].astype(BF16), vc_ref[0].astype(BF16), vn_ref[0].astype(BF16)], axis=0)
    q = q_ref[0] * (NA_HEAD_DIM ** -0.5)
    lane = lax.broadcasted_iota(jnp.int32, q.shape, 1)
    outs = []
    for hh in range(2):
        sel = (lane >= NA_HEAD_DIM * hh) & (lane < NA_HEAD_DIM * (hh + 1))
        qh = jnp.where(sel, q, 0.0).astype(BF16)
        s = lax.dot_general(qh, kw, (((1,), (1,)), ((), ())), preferred_element_type=F32)
        s = s + bias_ref[hh, 0]
        m = jnp.max(s, axis=-1, keepdims=True)
        e = jnp.exp(s - m)
        l = jnp.sum(e, axis=-1, keepdims=True)
        outs.append(jnp.dot(e.astype(BF16), vw, preferred_element_type=F32) / l)
    o_ref[0] = jnp.where(lane < NA_HEAD_DIM, outs[0], outs[1])


def na_attention(proj3, bias_pat):
    bn, L, _ = proj3.shape
    rows = L // GRID_W
    assert rows % NA_QROWS == 0 and rows >= 2 * NA_QROWS
    nrb = rows // NA_QROWS
    qb = 2 * LRU_WIDTH // LANES
    kb = qb + NA_WIDTH // LANES
    vb = kb + NA_WIDTH // LANES
    nbt = NA_BLOCK_TOKENS
    nht = nbt // 2

    def cur(col0):
        return pl.BlockSpec((1, nbt, LANES), lambda hp, rb, b: (b, rb, col0 + hp))

    def prev_half(col0):
        return pl.BlockSpec((1, nht, LANES), lambda hp, rb, b: (b, jnp.maximum(2 * rb - 1, 0), col0 + hp))

    def next_half(col0):
        return pl.BlockSpec((1, nht, LANES), lambda hp, rb, b: (b, jnp.minimum(2 * rb + 2, 2 * nrb - 1), col0 + hp))

    def pattern(rb):
        return jnp.where(rb == 0, 0, jnp.where(rb == nrb - 1, 2, 1))

    return pl.pallas_call(
        _na_kernel,
        grid=(NA_HEADS // 2, nrb, bn),
        in_specs=[cur(qb), prev_half(kb), cur(kb), next_half(kb), prev_half(vb), cur(vb), next_half(vb),
                  pl.BlockSpec((2, 1, nbt, NA_KROWS * GRID_W), lambda hp, rb, b: (hp, pattern(rb), 0, 0))],
        out_specs=pl.BlockSpec((1, nbt, LANES), lambda hp, rb, b: (b, rb, hp)),
        out_shape=jax.ShapeDtypeStruct((bn, L, NA_WIDTH), F32),
        compiler_params=_cparams("arbitrary", "arbitrary", "arbitrary"),
        name="na_attention",
    )(proj3, proj3, proj3, proj3, proj3, proj3, proj3, bias_pat)


def s5_prep(lam_re, lam_im, log_dt, b_re, b_im, c_re, c_im, levels):
    T = S5_CHUNK
    hp = lax.Precision.HIGHEST
    ks = np.arange(T + 1)
    per_dir = []
    for d in range(2):
        lr = jnp.minimum(lam_re[d].astype(F32), -1e-4)
        li = lam_im[d].astype(F32)
        dt = jnp.exp(log_dt[d].astype(F32))[:, None]
        mag = jnp.exp(lr * dt)
        ar = mag * jnp.cos(li * dt)
        ai = mag * jnp.sin(li * dt)
        den = lr * lr + li * li
        nr = ar - 1.0
        fr = (nr * lr + ai * li) / den
        fi = (ai * lr - nr * li) / den
        br_d = b_re[d].astype(F32)
        bi_d = b_im[d].astype(F32)
        bbr = fr[..., None] * br_d - fi[..., None] * bi_d
        bbi = fr[..., None] * bi_d + fi[..., None] * br_d
        pr = jnp.ones((T + 1,) + ar.shape, F32)
        pi = jnp.zeros((T + 1,) + ar.shape, F32)
        cr, ci = ar, ai
        for bit in range(int(T).bit_length()):
            sel = jnp.asarray(((ks >> bit) & 1) == 1)[:, None, None]
            pr, pi = jnp.where(sel, pr * cr - pi * ci, pr), jnp.where(sel, pr * ci + pi * cr, pi)
            cr, ci = cr * cr - ci * ci, 2.0 * cr * ci
        per_dir.append((pr, pi, bbr, bbi, c_re[d].astype(F32), c_im[d].astype(F32)))

    def lag_kernels(pr, pi, bbr, bbi, cre, cim):
        car = cre[None] * pr[:T, :, None, :] - cim[None] * pi[:T, :, None, :]
        cai = cre[None] * pi[:T, :, None, :] + cim[None] * pr[:T, :, None, :]
        return (jnp.einsum('kgop,gpc->kgoc', car, bbr, precision=hp)
                - jnp.einsum('kgop,gpc->kgoc', cai, bbi, precision=hp))

    kf = lag_kernels(*per_dir[0])
    kb = lag_kernels(*per_dir[1])
    kcat = jnp.concatenate([kb[:0:-1], (kf[0] + kb[0])[None], kf[1:]], axis=0)
    didx = np.arange(T)[None, :] - np.arange(T)[:, None] + T - 1
    m = kcat[didx].transpose(2, 0, 4, 1, 3).reshape(S5_GROUPS, S5_CW, S5_CW)

    prf, pif, bbrf, bbif, cref, cimf = per_dir[0]
    prb, pib, bbrb, bbib, creb, cimb = per_dir[1]
    jf = T - 1 - np.arange(T)
    jb = np.arange(T)

    def inj(pr, pi, bbr, bbi, e):
        re = pr[e][..., None] * bbr[None] - pi[e][..., None] * bbi[None]
        im = pr[e][..., None] * bbi[None] + pi[e][..., None] * bbr[None]
        return re, im

    sfr, sfi = inj(prf, pif, bbrf, bbif, jf)
    sbr, sbi = inj(prb, pib, bbrb, bbib, jb)
    p = jnp.concatenate([sfr, sfi, sbr, sbi], axis=2).transpose(1, 0, 3, 2).reshape(S5_GROUPS, S5_CW, S5_SW)

    def readout(pr, pi, cre, cim, e):
        re = cre[None] * pr[e][:, :, None, :] - cim[None] * pi[e][:, :, None, :]
        im = cre[None] * pi[e][:, :, None, :] + cim[None] * pr[e][:, :, None, :]
        return re, -im

    qfr, qfi = readout(prf, pif, cref, cimf, 1 + np.arange(T))
    qbr, qbi = readout(prb, pib, creb, cimb, T - np.arange(T))
    q = jnp.concatenate([qfr, qfi, qbr, qbi], axis=3).transpose(1, 3, 0, 2).reshape(S5_GROUPS, S5_SW, S5_CW)

    rows = []
    fr_, fi_ = prf[T], pif[T]
    br_, bi_ = prb[T], pib[T]
    for _ in range(levels):
        rows.append(jnp.concatenate([fr_, fr_, br_, br_], axis=1))
        rows.append(jnp.concatenate([-fi_, fi_, -bi_, bi_], axis=1))
        fr_, fi_ = fr_ * fr_ - fi_ * fi_, 2.0 * fr_ * fi_
        br_, bi_ = br_ * br_ - bi_ * bi_, 2.0 * br_ * bi_
    dec = jnp.stack(rows, axis=1)
    return m.astype(BF16), p.astype(BF16), q.astype(BF16), dec


def _s5_kernel(cps, levels, u_ref, m_ref, p_ref, q_ref, dec_ref, y_ref):
    u = u_ref[...]
    s = jnp.dot(u, p_ref[0], preferred_element_type=F32)
    rb = s.shape[0]
    pos = lax.broadcasted_iota(jnp.int32, (rb, LANES), 0) % cps
    sf = s[:, :LANES]
    sb = s[:, LANES:]
    half = S5_STATE
    for k in range(levels):
        sh = 1 << k
        a1 = dec_ref[0, 2 * k:2 * k + 1, :]
        a2 = dec_ref[0, 2 * k + 1:2 * k + 2, :]
        xs = jnp.where(pos >= sh, pltpu.roll(sf, sh, 0), 0.0)
        sf = sf + a1[:, :LANES] * xs + a2[:, :LANES] * pltpu.roll(xs, half, 1)
        xs = jnp.where(pos < cps - sh, pltpu.roll(sb, rb - sh, 0), 0.0)
        sb = sb + a1[:, LANES:] * xs + a2[:, LANES:] * pltpu.roll(xs, half, 1)
    pf = jnp.where(pos >= 1, pltpu.roll(sf, 1, 0), 0.0)
    pb = jnp.where(pos < cps - 1, pltpu.roll(sb, rb - 1, 0), 0.0)
    sprev = jnp.concatenate([pf, pb], axis=1).astype(BF16)
    y_ref[...] = (jnp.dot(u, m_ref[0], preferred_element_type=F32)
                  + jnp.dot(sprev, q_ref[0], preferred_element_type=F32))


def s5_chunk_scan(uc, m, p, q, dec, cps, levels):
    nchunks = uc.shape[0]
    rb = min(nchunks, 1024)
    assert nchunks % rb == 0 and rb % cps == 0 and (1 << levels) >= cps
    return pl.pallas_call(
        functools.partial(_s5_kernel, cps, levels),
        grid=(nchunks // rb, S5_GROUPS),
        in_specs=[pl.BlockSpec((rb, S5_CW), lambda i, g: (i, g)),
                  pl.BlockSpec((1, S5_CW, S5_CW), lambda i, g: (g, 0, 0)),
                  pl.BlockSpec((1, S5_CW, S5_SW), lambda i, g: (g, 0, 0)),
                  pl.BlockSpec((1, S5_SW, S5_CW), lambda i, g: (g, 0, 0)),
                  pl.BlockSpec((1, dec.shape[1], S5_SW), lambda i, g: (g, 0, 0))],
        out_specs=pl.BlockSpec((rb, S5_CW), lambda i, g: (i, g)),
        out_shape=jax.ShapeDtypeStruct((nchunks, S5_GROUPS * S5_CW), F32),
        compiler_params=_cparams("arbitrary", "arbitrary"),
        name="s5_chunk_scan",
    )(uc, m, p, q, dec)


def _s5_post_kernel(y_ref, u_ref, d_ref, w_ref, b_ref, o_ref):
    y = y_ref[...] + u_ref[...] * d_ref[...]
    z = jax.nn.gelu(y).astype(BF16)
    zz = jnp.dot(z, w_ref[...], preferred_element_type=F32) + b_ref[...]
    o_ref[...] = zz[:, :S5_WIDTH] * jax.nn.sigmoid(zz[:, S5_WIDTH:])


def s5_post(y, u, d_skip, glu_w_bf16, glu_b, tm=1024):
    n = y.shape[0]
    w = S5_WIDTH
    assert n % tm == 0
    return pl.pallas_call(
        _s5_post_kernel,
        grid=(n // tm,),
        in_specs=[pl.BlockSpec((tm, w), lambda i: (i, 0)), pl.BlockSpec((tm, w), lambda i: (i, 0)),
                  pl.BlockSpec((1, w), lambda i: (0, 0)), pl.BlockSpec((w, 2 * w), lambda i: (0, 0)),
                  pl.BlockSpec((1, 2 * w), lambda i: (0, 0))],
        out_specs=pl.BlockSpec((tm, w), lambda i: (i, 0)),
        out_shape=jax.ShapeDtypeStruct((n, w), F32),
        compiler_params=_cparams("arbitrary"),
        name="s5_post",
    )(y, u, d_skip.reshape(1, w), glu_w_bf16, glu_b.reshape(1, 2 * w))


def s5_mixer(u, s5w, d_skip, glu_w_bf16, glu_b, seq_len):
    n = u.shape[0]
    T = S5_CHUNK
    m, p, q, dec = s5w
    cps = seq_len // T
    levels = int(cps).bit_length() - 1
    assert (1 << levels) == cps and seq_len % T == 0
    uc = u.reshape(n // T, T, S5_GROUPS, S5_GROUP).transpose(0, 2, 1, 3).reshape(n // T, S5_GROUPS * S5_CW)
    yc = s5_chunk_scan(uc.astype(BF16), m, p, q, dec, cps, levels)
    y = yc.reshape(n // T, S5_GROUPS, T, S5_GROUP).transpose(0, 2, 1, 3).reshape(n, S5_WIDTH)
    return s5_post(y, u, d_skip, glu_w_bf16, glu_b)


def _out_proj_kernel(ya_ref, yb_ref, yc_ref, x_ref, bg_ref, w_ref, g2_ref, rwh_ref, rwl_ref,
                     x1_ref, hn_ref, aff_ref):
    o1 = LRU_WIDTH
    o2 = LRU_WIDTH + NA_WIDTH
    ma = _rms(ya_ref[...], bg_ref[:, 0:o1]).astype(BF16)
    mb = _rms(yb_ref[...], bg_ref[:, o1:o2]).astype(BF16)
    mc = _rms(yc_ref[...], bg_ref[:, o2:MIX_WIDTH]).astype(BF16)
    x1 = (x_ref[...]
          + jnp.dot(ma, w_ref[0, 0:o1, :], preferred_element_type=F32)
          + jnp.dot(mb, w_ref[0, o1:o2, :], preferred_element_type=F32)
          + jnp.dot(mc, w_ref[0, o2:MIX_WIDTH, :], preferred_element_type=F32))
    x1_ref[...] = x1
    h2 = _rms(x1, g2_ref[...])
    hb = h2.astype(BF16)
    hn_ref[...] = hb
    lo = (h2 - hb.astype(F32)).astype(BF16)
    logits = (jnp.dot(hb, rwh_ref[...], preferred_element_type=F32)
              + jnp.dot(hb, rwl_ref[...], preferred_element_type=F32)
              + jnp.dot(lo, rwh_ref[...], preferred_element_type=F32))
    lane = lax.broadcasted_iota(jnp.int32, logits.shape, 1)
    logits = jnp.where(lane < N_EXPERTS, logits, -1e30)
    m = jnp.max(logits, axis=-1, keepdims=True)
    e = jnp.exp(logits - m)
    aff_ref[...] = e / jnp.sum(e, axis=-1, keepdims=True)


def out_proj(ya, yb, yc, x2d, branch_g, w_out_bf16, layer, norm2_g, rw_hi, rw_lo, tm=256):
    n, d = x2d.shape
    assert n % tm == 0
    row = lambda i: (i, 0)
    fixed = lambda i: (0, 0)
    return pl.pallas_call(
        _out_proj_kernel,
        grid=(n // tm,),
        in_specs=[pl.BlockSpec((tm, LRU_WIDTH), row), pl.BlockSpec((tm, NA_WIDTH), row),
                  pl.BlockSpec((tm, S5_WIDTH), row), pl.BlockSpec((tm, d), row),
                  pl.BlockSpec((1, MIX_WIDTH), fixed), pl.BlockSpec((1, MIX_WIDTH, d), lambda i: (layer, 0, 0)),
                  pl.BlockSpec((1, d), fixed), pl.BlockSpec((d, LANES), fixed), pl.BlockSpec((d, LANES), fixed)],
        out_specs=[pl.BlockSpec((tm, d), row), pl.BlockSpec((tm, d), row), pl.BlockSpec((tm, LANES), row)],
        out_shape=[jax.ShapeDtypeStruct((n, d), F32), jax.ShapeDtypeStruct((n, d), BF16),
                   jax.ShapeDtypeStruct((n, LANES), F32)],
        compiler_params=_cparams("arbitrary"),
        name="out_proj_router",
    )(ya, yb, yc, x2d, branch_g.reshape(1, MIX_WIDTH), w_out_bf16, norm2_g.reshape(1, d), rw_hi, rw_lo)


def _ffn_kernel(nf, x_ref, wg_ref, wu_ref, wd_ref, g_ref, o_ref):
    f = pl.program_id(2)
    x = x_ref[0]
    gate = jnp.dot(x, wg_ref[0, 0], preferred_element_type=F32)
    up = jnp.dot(x, wu_ref[0, 0], preferred_element_type=F32)
    h = (gate * jax.nn.sigmoid(gate) * up).astype(BF16)
    contrib = jnp.dot(h, wd_ref[0, 0], preferred_element_type=F32)

    @pl.when(f == 0)
    def _():
        o_ref[0] = contrib

    @pl.when(f > 0)
    def _():
        o_ref[0] += contrib

    @pl.when(f == nf - 1)
    def _():
        o_ref[0] = o_ref[0] * g_ref[0]


def expert_ffn(xe, wg, wu, wd, layer, g3, tm=1024, tf=512):
    ne, cap, d = xe.shape
    ff = wg.shape[3]
    tm = min(tm, cap)
    assert cap % tm == 0 and ff % tf == 0
    nf = ff // tf
    return pl.pallas_call(
        functools.partial(_ffn_kernel, nf),
        grid=(ne, cap // tm, nf),
        in_specs=[pl.BlockSpec((1, tm, d), lambda e, i, f: (e, i, 0)),
                  pl.BlockSpec((1, 1, d, tf), lambda e, i, f: (layer, e, 0, f)),
                  pl.BlockSpec((1, 1, d, tf), lambda e, i, f: (layer, e, 0, f)),
                  pl.BlockSpec((1, 1, tf, d), lambda e, i, f: (layer, e, f, 0)),
                  pl.BlockSpec((1, tm, 1), lambda e, i, f: (e, i, 0))],
        out_specs=pl.BlockSpec((1, tm, d), lambda e, i, f: (e, i, 0)),
        out_shape=jax.ShapeDtypeStruct((ne, cap, d), F32),
        compiler_params=_cparams("arbitrary", "arbitrary", "arbitrary"),
        name="expert_ffn",
    )(xe, wg, wu, wd, g3)


def _final_norm_kernel(x_ref, g_ref, o_ref):
    o_ref[...] = _rms(x_ref[...], g_ref[...])


def final_rmsnorm(x2d, g, tm=1024):
    n, d = x2d.shape
    tm = min(tm, n)
    assert n % tm == 0
    return pl.pallas_call(
        _final_norm_kernel,
        grid=(n // tm,),
        in_specs=[pl.BlockSpec((tm, d), lambda i: (i, 0)), pl.BlockSpec((1, d), lambda i: (0, 0))],
        out_specs=pl.BlockSpec((tm, d), lambda i: (i, 0)),
        out_shape=jax.ShapeDtypeStruct((n, d), F32),
        compiler_params=_cparams("arbitrary"),
        name="final_rmsnorm",
    )(x2d, g.reshape(1, d))


def prep_layer(norm1_g, conv_w, conv_b, gate_a_w, gate_a_b, gate_x_w, gate_x_b, lru_lambda, rpb,
               s5_lambda_re, s5_lambda_im, s5_log_dt, s5_b_re, s5_b_im, s5_c_re, s5_c_im, s5_d, glu_w, glu_b,
               branch_g, norm2_g, router_w, s5_levels):
    d = router_w.shape[0]
    rw = jnp.zeros((d, LANES), F32).at[:, :N_EXPERTS].set(router_w.astype(F32))
    rw_hi = rw.astype(BF16)
    rw_lo = (rw - rw_hi.astype(F32)).astype(BF16)
    return dict(
        norm1_g=norm1_g, conv_w=conv_w.astype(F32), conv_b=conv_b.astype(F32),
        wa=gate_a_w.astype(BF16), wx=gate_x_w.astype(BF16), ba=gate_a_b.astype(F32), bx=gate_x_b.astype(F32),
        c8=-LRU_C * jax.nn.softplus(-lru_lambda.astype(F32)),
        bias_pat=na_bias_patterns(rpb),
        s5w=s5_prep(s5_lambda_re, s5_lambda_im, s5_log_dt, s5_b_re, s5_b_im, s5_c_re, s5_c_im, s5_levels),
        s5_d=s5_d.astype(F32), glu_w=glu_w.astype(BF16), glu_b=glu_b.astype(F32),
        branch_g=branch_g.astype(F32), norm2_g=norm2_g.astype(F32), rw_hi=rw_hi, rw_lo=rw_lo)


def trunk_layer(x, lw, big, layer):
    bn, L, d = x.shape
    n = bn * L
    x2d = x.reshape(n, d)
    proj = in_proj(x2d, lw['norm1_g'], big['w_in'], layer)
    proj3 = proj.reshape(bn, L, IN_WIDTH)
    hf = lru_scan(proj3, lw['conv_w'], lw['conv_b'], lw['wa'][0], lw['wx'][0], lw['ba'][0], lw['bx'][0],
                  lw['c8'][0], rev=False)
    ya = lru_scan(proj3, lw['conv_w'], lw['conv_b'], lw['wa'][1], lw['wx'][1], lw['ba'][1], lw['bx'][1],
                  lw['c8'][1], rev=True, hf=hf)
    yb = na_attention(proj3, lw['bias_pat'])
    u = proj[:, IN_WIDTH - S5_WIDTH:]
    yc = s5_mixer(u, lw['s5w'], lw['s5_d'], lw['glu_w'], lw['glu_b'], L)
    x1, hn, aff = out_proj(ya.reshape(n, LRU_WIDTH), yb.reshape(n, NA_WIDTH), yc, x2d, lw['branch_g'],
                           big['w_out'], layer, lw['norm2_g'], lw['rw_hi'], lw['rw_lo'])
    cap = max(1, EC_CAPACITY * n // N_EXPERTS)
    g, idx = lax.top_k(aff[:, :N_EXPERTS].T, cap)
    xe = hn[idx]
    ye = expert_ffn(xe, big['w_gate'], big['w_up'], big['w_down'], layer, g[..., None])
    x2 = x1.at[idx.reshape(-1)].add(ye.reshape(-1, d))
    return x2.reshape(bn, L, d)


def kernel(x_prompt, x_sample, norm1_g, w_in, conv_w, conv_b, gate_a_w, gate_a_b, gate_x_w, gate_x_b,
           lru_lambda, rpb, s5_lambda_re, s5_lambda_im, s5_log_dt, s5_b_re, s5_b_im, s5_c_re, s5_c_im,
           s5_d, glu_w, glu_b, branch_g, w_out, norm2_g, router_w, w_gate, w_up, w_down, final_g):
    xp = x_prompt
    xs = x_sample
    max_len = max(xp.shape[1], xs.shape[1])
    s5_levels = int(max_len // S5_CHUNK).bit_length() - 1
    big = dict(w_in=w_in.astype(BF16), w_out=w_out.astype(BF16), w_gate=w_gate.astype(BF16),
               w_up=w_up.astype(BF16), w_down=w_down.astype(BF16))
    for l in range(DEPTH):
        lw = prep_layer(norm1_g[l], conv_w[l], conv_b[l], gate_a_w[l], gate_a_b[l], gate_x_w[l],
                        gate_x_b[l], lru_lambda[l], rpb[l], s5_lambda_re[l], s5_lambda_im[l], s5_log_dt[l],
                        s5_b_re[l], s5_b_im[l], s5_c_re[l], s5_c_im[l], s5_d[l], glu_w[l], glu_b[l],
                        branch_g[l], norm2_g[l], router_w[l], s5_levels)
        xp = trunk_layer(xp, lw, big, l)
        xs = trunk_layer(xs, lw, big, l)
    d = xp.shape[-1]
    y_prompt = final_rmsnorm(xp.reshape(-1, d), final_g).reshape(xp.shape)
    y_sample = final_rmsnorm(xs.reshape(-1, d), final_g).reshape(xs.shape)
    return (y_prompt, y_sample)
```

```python
import functools

import jax
import jax.numpy as jnp
import numpy as np
from jax import lax
from jax.experimental import pallas as pl
from jax.experimental.pallas import tpu as pltpu

DEPTH = 4
LRU_WIDTH = 768
LRU_BLOCKS = 6
LRU_BLOCK = LRU_WIDTH // LRU_BLOCKS
CONV_W = 4
LRU_C = 8.0
NA_HEADS = 12
NA_HEAD_DIM = 64
NA_WIDTH = NA_HEADS * NA_HEAD_DIM
GRID_W = 64
WIN_ROWS = 8
WIN_COLS = 16
S5_WIDTH = 512
S5_GROUP = 16
S5_GROUPS = S5_WIDTH // S5_GROUP
S5_STATE = 64
MIX_WIDTH = LRU_WIDTH + NA_WIDTH + S5_WIDTH
IN_WIDTH = 2 * LRU_WIDTH + 3 * NA_WIDTH + S5_WIDTH
N_EXPERTS = 16
EC_CAPACITY = 2
EPS = 1e-6

SUBLANES = 8
LANES = 128
VMEM_LIMIT_BYTES = 52 * 1024 * 1024

S5_CHUNK = 32
S5_CW = S5_CHUNK * S5_GROUP
S5_SW = 4 * S5_STATE
NA_QROWS = 8
NA_BLOCK_TOKENS = NA_QROWS * GRID_W
NA_KROWS = NA_QROWS + WIN_ROWS

F32 = jnp.float32
BF16 = jnp.bfloat16


def _cparams(*sem):
    return pltpu.CompilerParams(dimension_semantics=sem, vmem_limit_bytes=VMEM_LIMIT_BYTES)


def _rms(v, g):
    ms = jnp.mean(v * v, axis=-1, keepdims=True)
    return v * lax.rsqrt(ms + EPS) * g


def _in_proj_kernel(x_ref, g_ref, w_ref, o_ref):
    h = _rms(x_ref[...], g_ref[...]).astype(BF16)
    o_ref[...] = jnp.dot(h, w_ref[0], preferred_element_type=F32)


def in_proj(x2d, g, w_bf16, layer, tm=512):
    n, d = x2d.shape
    nw = w_bf16.shape[2]
    tn = nw // 2
    assert n % tm == 0 and tn % LANES == 0
    return pl.pallas_call(
        _in_proj_kernel,
        grid=(2, n // tm),
        in_specs=[pl.BlockSpec((tm, d), lambda j, i: (i, 0)),
                  pl.BlockSpec((1, d), lambda j, i: (0, 0)),
                  pl.BlockSpec((1, d, tn), lambda j, i: (layer, 0, j))],
        out_specs=pl.BlockSpec((tm, tn), lambda j, i: (i, j)),
        out_shape=jax.ShapeDtypeStruct((n, nw), F32),
        compiler_params=_cparams("arbitrary", "arbitrary"),
        name="in_proj",
    )(x2d, g.reshape(1, d), w_bf16)


def _lru_kernel(rev, tl, nt, *refs):
    if rev:
        (xm_ref, xp_ref, xn_ref, cw_ref, cb_ref, wa_ref, wx_ref, ba_ref, bx_ref, c8_ref, ga_ref, hf_ref,
         o_ref, ext_ref, a_ref, b_ref, carry_ref) = refs
    else:
        (xm_ref, xp_ref, xn_ref, cw_ref, cb_ref, wa_ref, wx_ref, ba_ref, bx_ref, c8_ref,
         o_ref, ext_ref, a_ref, b_ref, carry_ref) = refs
    i = pl.program_id(1)
    ci = (nt - 1 - i) if rev else i
    w = LRU_WIDTH
    nb = tl // SUBLANES

    xm = xm_ref[0]
    ext_ref[SUBLANES:SUBLANES + tl, :] = xm
    ext_ref[0:SUBLANES, :] = jnp.where(ci > 0, xp_ref[0], 0.0)
    ext_ref[SUBLANES + tl:2 * SUBLANES + tl, :] = jnp.where(ci < nt - 1, xn_ref[0], 0.0)
    cw = cw_ref[...]
    xc = (cw[0:1] * ext_ref[SUBLANES - 2:SUBLANES - 2 + tl, :]
          + cw[1:2] * ext_ref[SUBLANES - 1:SUBLANES - 1 + tl, :]
          + cw[2:3] * xm
          + cw[3:4] * ext_ref[SUBLANES + 1:SUBLANES + 1 + tl, :]) + cb_ref[...]

    xcb = xc.astype(BF16)
    r_parts = []
    i_parts = []
    for h in range(LRU_BLOCKS):
        xb = xcb[:, h * LRU_BLOCK:(h + 1) * LRU_BLOCK]
        r_parts.append(jnp.dot(xb, wa_ref[h], preferred_element_type=F32))
        i_parts.append(jnp.dot(xb, wx_ref[h], preferred_element_type=F32))
    r = jax.nn.sigmoid(jnp.concatenate(r_parts, axis=1) + ba_ref[...])
    ig = jax.nn.sigmoid(jnp.concatenate(i_parts, axis=1) + bx_ref[...])
    a = jnp.exp(r * c8_ref[...])
    b = jnp.sqrt(1.0 - a * a) * (ig * xc)

    row = lax.broadcasted_iota(jnp.int32, (tl, w), 0) % SUBLANES
    for s in (1, 2, 4):
        if rev:
            m = row < SUBLANES - s
            a_sh = pltpu.roll(a, tl - s, 0)
            b_sh = pltpu.roll(b, tl - s, 0)
        else:
            m = row >= s
            a_sh = pltpu.roll(a, s, 0)
            b_sh = pltpu.roll(b, s, 0)
        b = b + a * jnp.where(m, b_sh, 0.0)
        a = a * jnp.where(m, a_sh, 1.0)
    a_ref[...] = a
    b_ref[...] = b

    @pl.when(i == 0)
    def _():
        carry_ref[...] = jnp.zeros_like(carry_ref)

    def body(j, c):
        jj = (nb - 1 - j) if rev else j
        st = pl.multiple_of(jj * SUBLANES, SUBLANES)
        h = b_ref[pl.ds(st, SUBLANES), :] + a_ref[pl.ds(st, SUBLANES), :] * c
        b_ref[pl.ds(st, SUBLANES), :] = h
        edge = h[0:1, :] if rev else h[SUBLANES - 1:SUBLANES, :]
        return jnp.broadcast_to(edge, (SUBLANES, w))

    carry_ref[...] = lax.fori_loop(0, nb, body, carry_ref[...])

    if rev:
        o_ref[0] = (hf_ref[0] + b_ref[...]) * jax.nn.gelu(ga_ref[0])
    else:
        o_ref[0] = b_ref[...]


def lru_scan(proj3, conv_w, conv_b, wa, wx, ba, bx, c8, rev, hf=None, tl=512):
    bn, L, _ = proj3.shape
    w = LRU_WIDTH
    assert L % tl == 0
    nt = L // tl
    tb = tl // SUBLANES
    nblk8 = L // SUBLANES

    def cidx(i):
        return (nt - 1 - i) if rev else i

    in_specs = [
        pl.BlockSpec((1, tl, w), lambda b, i: (b, cidx(i), 0)),
        pl.BlockSpec((1, SUBLANES, w), lambda b, i: (b, jnp.maximum(cidx(i) * tb - 1, 0), 0)),
        pl.BlockSpec((1, SUBLANES, w), lambda b, i: (b, jnp.minimum((cidx(i) + 1) * tb, nblk8 - 1), 0)),
        pl.BlockSpec((CONV_W, w), lambda b, i: (0, 0)),
        pl.BlockSpec((1, w), lambda b, i: (0, 0)),
        pl.BlockSpec((LRU_BLOCKS, LRU_BLOCK, LRU_BLOCK), lambda b, i: (0, 0, 0)),
        pl.BlockSpec((LRU_BLOCKS, LRU_BLOCK, LRU_BLOCK), lambda b, i: (0, 0, 0)),
        pl.BlockSpec((1, w), lambda b, i: (0, 0)),
        pl.BlockSpec((1, w), lambda b, i: (0, 0)),
        pl.BlockSpec((1, w), lambda b, i: (0, 0)),
    ]
    args = [proj3, proj3, proj3, conv_w, conv_b.reshape(1, w), wa, wx, ba.reshape(1, w), bx.reshape(1, w),
            c8.reshape(1, w)]
    if rev:
        in_specs += [pl.BlockSpec((1, tl, w), lambda b, i: (b, cidx(i), 1)),
                     pl.BlockSpec((1, tl, w), lambda b, i: (b, cidx(i), 0))]
        args += [proj3, hf]
    return pl.pallas_call(
        functools.partial(_lru_kernel, rev, tl, nt),
        grid=(bn, nt),
        in_specs=in_specs,
        out_specs=pl.BlockSpec((1, tl, w), lambda b, i: (b, cidx(i), 0)),
        out_shape=jax.ShapeDtypeStruct((bn, L, w), F32),
        scratch_shapes=[pltpu.VMEM((tl + 2 * SUBLANES, w), F32), pltpu.VMEM((tl, w), F32),
                        pltpu.VMEM((tl, w), F32), pltpu.VMEM((SUBLANES, w), F32)],
        compiler_params=_cparams("arbitrary", "arbitrary"),
        name="lru_scan_bwd" if rev else "lru_scan_fwd",
    )(*args)


def na_bias_patterns(rpb):
    half = WIN_ROWS // 2
    qc = np.arange(GRID_W)[:, None]
    kc = np.arange(GRID_W)[None, :]
    cs = np.clip(qc - WIN_COLS // 2, 0, GRID_W - WIN_COLS)
    colmask = (kc >= cs) & (kc < cs + WIN_COLS)
    dc = np.clip(kc - qc + WIN_COLS - 1, 0, 2 * WIN_COLS - 2)
    rl = np.arange(NA_QROWS)
    kr = np.arange(NA_KROWS)
    w0 = np.stack([np.maximum(rl - half, 0), rl - half, np.minimum(rl - half, 0)])
    rel = kr[None, None, :] - half - w0[:, :, None]
    valid = (rel >= 0) & (rel < WIN_ROWS)
    dr = np.clip(kr[None, :] - rl[:, None] + WIN_ROWS - 1 - half, 0, 2 * WIN_ROWS - 2)
    vals = rpb.astype(F32)[:, dr][:, :, :, dc]
    mask = valid[:, :, :, None, None] & colmask[None, None, None]
    full = jnp.where(jnp.asarray(mask)[None], vals[:, None], -1e30)
    return full.transpose(0, 1, 2, 4, 3, 5).reshape(NA_HEADS, 3, NA_BLOCK_TOKENS, NA_KROWS * GRID_W)


def _na_kernel(q_ref, kp_ref, kc_ref, kn_ref, vp_ref, vc_ref, vn_ref, bias_ref, o_ref):
    kw = jnp.concatenate([kp_ref[0].astype(BF16), kc_ref[0].astype(BF16), kn_ref[0].astype(BF16)], axis=0)
    vw = jnp.concatenate([vp_ref[0].astype(BF16), vc_ref[0].astype(BF16), vn_ref[0].astype(BF16)], axis=0)
    nq = NA_BLOCK_TOKENS // 2
    nk = (NA_KROWS - WIN_ROWS // 2) * GRID_W
    lane = lax.broadcasted_iota(jnp.int32, (nq, LANES), 1)
    for half in range(2):
        q0 = half * nq
        k0 = half * (WIN_ROWS // 2) * GRID_W
        q = q_ref[0, q0:q0 + nq, :] * (NA_HEAD_DIM ** -0.5)
        kh = kw[k0:k0 + nk]
        vh = vw[k0:k0 + nk]
        outs = []
        for hh in range(2):
            sel = (lane >= NA_HEAD_DIM * hh) & (lane < NA_HEAD_DIM * (hh + 1))
            qh = jnp.where(sel, q, 0.0).astype(BF16)
            s = lax.dot_general(qh, kh, (((1,), (1,)), ((), ())), preferred_element_type=F32)
            s = s + bias_ref[hh, 0, q0:q0 + nq, k0:k0 + nk]
            m = jnp.max(s, axis=-1, keepdims=True)
            e = jnp.exp(s - m)
            l = jnp.sum(e, axis=-1, keepdims=True)
            outs.append(jnp.dot(e.astype(BF16), vh, preferred_element_type=F32) / l)
        o_ref[0, q0:q0 + nq, :] = jnp.where(lane < NA_HEAD_DIM, outs[0], outs[1])


def na_attention(proj3, bias_pat):
    bn, L, _ = proj3.shape
    rows = L // GRID_W
    assert rows % NA_QROWS == 0 and rows >= 2 * NA_QROWS
    nrb = rows // NA_QROWS
    qb = 2 * LRU_WIDTH // LANES
    kb = qb + NA_WIDTH // LANES
    vb = kb + NA_WIDTH // LANES
    nbt = NA_BLOCK_TOKENS
    nht = nbt // 2

    def cur(col0):
        return pl.BlockSpec((1, nbt, LANES), lambda hp, rb, b: (b, rb, col0 + hp))

    def prev_half(col0):
        return pl.BlockSpec((1, nht, LANES), lambda hp, rb, b: (b, jnp.maximum(2 * rb - 1, 0), col0 + hp))

    def next_half(col0):
        return pl.BlockSpec((1, nht, LANES), lambda hp, rb, b: (b, jnp.minimum(2 * rb + 2, 2 * nrb - 1), col0 + hp))

    def pattern(rb):
        return jnp.where(rb == 0, 0, jnp.where(rb == nrb - 1, 2, 1))

    return pl.pallas_call(
        _na_kernel,
        grid=(NA_HEADS // 2, nrb, bn),
        in_specs=[cur(qb), prev_half(kb), cur(kb), next_half(kb), prev_half(vb), cur(vb), next_half(vb),
                  pl.BlockSpec((2, 1, nbt, NA_KROWS * GRID_W), lambda hp, rb, b: (hp, pattern(rb), 0, 0))],
        out_specs=pl.BlockSpec((1, nbt, LANES), lambda hp, rb, b: (b, rb, hp)),
        out_shape=jax.ShapeDtypeStruct((bn, L, NA_WIDTH), F32),
        compiler_params=_cparams("arbitrary", "arbitrary", "arbitrary"),
        name="na_attention",
    )(proj3, proj3, proj3, proj3, proj3, proj3, proj3, bias_pat)


def s5_prep(lam_re, lam_im, log_dt, b_re, b_im, c_re, c_im, levels):
    T = S5_CHUNK
    hp = lax.Precision.HIGHEST
    ks = np.arange(T + 1)
    per_dir = []
    for d in range(2):
        lr = jnp.minimum(lam_re[d].astype(F32), -1e-4)
        li = lam_im[d].astype(F32)
        dt = jnp.exp(log_dt[d].astype(F32))[:, None]
        mag = jnp.exp(lr * dt)
        ar = mag * jnp.cos(li * dt)
        ai = mag * jnp.sin(li * dt)
        den = lr * lr + li * li
        nr = ar - 1.0
        fr = (nr * lr + ai * li) / den
        fi = (ai * lr - nr * li) / den
        br_d = b_re[d].astype(F32)
        bi_d = b_im[d].astype(F32)
        bbr = fr[..., None] * br_d - fi[..., None] * bi_d
        bbi = fr[..., None] * bi_d + fi[..., None] * br_d
        pr = jnp.ones((T + 1,) + ar.shape, F32)
        pi = jnp.zeros((T + 1,) + ar.shape, F32)
        cr, ci = ar, ai
        for bit in range(int(T).bit_length()):
            sel = jnp.asarray(((ks >> bit) & 1) == 1)[:, None, None]
            pr, pi = jnp.where(sel, pr * cr - pi * ci, pr), jnp.where(sel, pr * ci + pi * cr, pi)
            cr, ci = cr * cr - ci * ci, 2.0 * cr * ci
        per_dir.append((pr, pi, bbr, bbi, c_re[d].astype(F32), c_im[d].astype(F32)))

    def lag_kernels(pr, pi, bbr, bbi, cre, cim):
        car = cre[None] * pr[:T, :, None, :] - cim[None] * pi[:T, :, None, :]
        cai = cre[None] * pi[:T, :, None, :] + cim[None] * pr[:T, :, None, :]
        return (jnp.einsum('kgop,gpc->kgoc', car, bbr, precision=hp)
                - jnp.einsum('kgop,gpc->kgoc', cai, bbi, precision=hp))

    kf = lag_kernels(*per_dir[0])
    kb = lag_kernels(*per_dir[1])
    kcat = jnp.concatenate([kb[:0:-1], (kf[0] + kb[0])[None], kf[1:]], axis=0)
    didx = np.arange(T)[None, :] - np.arange(T)[:, None] + T - 1
    m = kcat[didx].transpose(2, 0, 4, 1, 3).reshape(S5_GROUPS, S5_CW, S5_CW)

    prf, pif, bbrf, bbif, cref, cimf = per_dir[0]
    prb, pib, bbrb, bbib, creb, cimb = per_dir[1]
    jf = T - 1 - np.arange(T)
    jb = np.arange(T)

    def inj(pr, pi, bbr, bbi, e):
        re = pr[e][..., None] * bbr[None] - pi[e][..., None] * bbi[None]
        im = pr[e][..., None] * bbi[None] + pi[e][..., None] * bbr[None]
        return re, im

    sfr, sfi = inj(prf, pif, bbrf, bbif, jf)
    sbr, sbi = inj(prb, pib, bbrb, bbib, jb)
    p = jnp.concatenate([sfr, sfi, sbr, sbi], axis=2).transpose(1, 0, 3, 2).reshape(S5_GROUPS, S5_CW, S5_SW)

    def readout(pr, pi, cre, cim, e):
        re = cre[None] * pr[e][:, :, None, :] - cim[None] * pi[e][:, :, None, :]
        im = cre[None] * pi[e][:, :, None, :] + cim[None] * pr[e][:, :, None, :]
        return re, -im

    qfr, qfi = readout(prf, pif, cref, cimf, 1 + np.arange(T))
    qbr, qbi = readout(prb, pib, creb, cimb, T - np.arange(T))
    q = jnp.concatenate([qfr, qfi, qbr, qbi], axis=3).transpose(1, 3, 0, 2).reshape(S5_GROUPS, S5_SW, S5_CW)

    rows = []
    fr_, fi_ = prf[T], pif[T]
    br_, bi_ = prb[T], pib[T]
    for _ in range(levels):
        rows.append(jnp.concatenate([fr_, fr_, br_, br_], axis=1))
        rows.append(jnp.concatenate([-fi_, fi_, -bi_, bi_], axis=1))
        fr_, fi_ = fr_ * fr_ - fi_ * fi_, 2.0 * fr_ * fi_
        br_, bi_ = br_ * br_ - bi_ * bi_, 2.0 * br_ * bi_
    dec = jnp.stack(rows, axis=1)
    return m.astype(BF16), p.astype(BF16), q.astype(BF16), dec


def _s5_kernel(cps, levels, u_ref, m_ref, p_ref, q_ref, dec_ref, y_ref):
    u = u_ref[...]
    s = jnp.dot(u, p_ref[0], preferred_element_type=F32)
    rb = s.shape[0]
    pos = lax.broadcasted_iota(jnp.int32, (rb, LANES), 0) % cps
    sf = s[:, :LANES]
    sb = s[:, LANES:]
    half = S5_STATE
    for k in range(levels):
        sh = 1 << k
        a1 = dec_ref[0, 2 * k:2 * k + 1, :]
        a2 = dec_ref[0, 2 * k + 1:2 * k + 2, :]
        xs = jnp.where(pos >= sh, pltpu.roll(sf, sh, 0), 0.0)
        sf = sf + a1[:, :LANES] * xs + a2[:, :LANES] * pltpu.roll(xs, half, 1)
        xs = jnp.where(pos < cps - sh, pltpu.roll(sb, rb - sh, 0), 0.0)
        sb = sb + a1[:, LANES:] * xs + a2[:, LANES:] * pltpu.roll(xs, half, 1)
    pf = jnp.where(pos >= 1, pltpu.roll(sf, 1, 0), 0.0)
    pb = jnp.where(pos < cps - 1, pltpu.roll(sb, rb - 1, 0), 0.0)
    sprev = jnp.concatenate([pf, pb], axis=1).astype(BF16)
    y_ref[...] = (jnp.dot(u, m_ref[0], preferred_element_type=F32)
                  + jnp.dot(sprev, q_ref[0], preferred_element_type=F32))


def s5_chunk_scan(uc, m, p, q, dec, cps, levels):
    nchunks = uc.shape[0]
    rb = min(nchunks, 1024)
    assert nchunks % rb == 0 and rb % cps == 0 and (1 << levels) >= cps
    return pl.pallas_call(
        functools.partial(_s5_kernel, cps, levels),
        grid=(nchunks // rb, S5_GROUPS),
        in_specs=[pl.BlockSpec((rb, S5_CW), lambda i, g: (i, g)),
                  pl.BlockSpec((1, S5_CW, S5_CW), lambda i, g: (g, 0, 0)),
                  pl.BlockSpec((1, S5_CW, S5_SW), lambda i, g: (g, 0, 0)),
                  pl.BlockSpec((1, S5_SW, S5_CW), lambda i, g: (g, 0, 0)),
                  pl.BlockSpec((1, dec.shape[1], S5_SW), lambda i, g: (g, 0, 0))],
        out_specs=pl.BlockSpec((rb, S5_CW), lambda i, g: (i, g)),
        out_shape=jax.ShapeDtypeStruct((nchunks, S5_GROUPS * S5_CW), F32),
        compiler_params=_cparams("arbitrary", "arbitrary"),
        name="s5_chunk_scan",
    )(uc, m, p, q, dec)


def _s5_post_kernel(y_ref, u_ref, d_ref, w_ref, b_ref, o_ref):
    y = y_ref[...] + u_ref[...] * d_ref[...]
    z = jax.nn.gelu(y).astype(BF16)
    zz = jnp.dot(z, w_ref[...], preferred_element_type=F32) + b_ref[...]
    o_ref[...] = zz[:, :S5_WIDTH] * jax.nn.sigmoid(zz[:, S5_WIDTH:])


def s5_post(y, u, d_skip, glu_w_bf16, glu_b, tm=1024):
    n = y.shape[0]
    w = S5_WIDTH
    assert n % tm == 0
    return pl.pallas_call(
        _s5_post_kernel,
        grid=(n // tm,),
        in_specs=[pl.BlockSpec((tm, w), lambda i: (i, 0)), pl.BlockSpec((tm, w), lambda i: (i, 0)),
                  pl.BlockSpec((1, w), lambda i: (0, 0)), pl.BlockSpec((w, 2 * w), lambda i: (0, 0)),
                  pl.BlockSpec((1, 2 * w), lambda i: (0, 0))],
        out_specs=pl.BlockSpec((tm, w), lambda i: (i, 0)),
        out_shape=jax.ShapeDtypeStruct((n, w), F32),
        compiler_params=_cparams("arbitrary"),
        name="s5_post",
    )(y, u, d_skip.reshape(1, w), glu_w_bf16, glu_b.reshape(1, 2 * w))


def s5_mixer(u, s5w, d_skip, glu_w_bf16, glu_b, seq_len):
    n = u.shape[0]
    T = S5_CHUNK
    m, p, q, dec = s5w
    cps = seq_len // T
    levels = int(cps).bit_length() - 1
    assert (1 << levels) == cps and seq_len % T == 0
    uc = u.reshape(n // T, T, S5_GROUPS, S5_GROUP).transpose(0, 2, 1, 3).reshape(n // T, S5_GROUPS * S5_CW)
    yc = s5_chunk_scan(uc.astype(BF16), m, p, q, dec, cps, levels)
    y = yc.reshape(n // T, S5_GROUPS, T, S5_GROUP).transpose(0, 2, 1, 3).reshape(n, S5_WIDTH)
    return s5_post(y, u, d_skip, glu_w_bf16, glu_b)


def _out_proj_kernel(ya_ref, yb_ref, yc_ref, x_ref, bg_ref, w_ref, g2_ref, rwh_ref, rwl_ref,
                     x1_ref, hn_ref, aff_ref):
    o1 = LRU_WIDTH
    o2 = LRU_WIDTH + NA_WIDTH
    ma = _rms(ya_ref[...], bg_ref[:, 0:o1]).astype(BF16)
    mb = _rms(yb_ref[...], bg_ref[:, o1:o2]).astype(BF16)
    mc = _rms(yc_ref[...], bg_ref[:, o2:MIX_WIDTH]).astype(BF16)
    x1 = (x_ref[...]
          + jnp.dot(ma, w_ref[0, 0:o1, :], preferred_element_type=F32)
          + jnp.dot(mb, w_ref[0, o1:o2, :], preferred_element_type=F32)
          + jnp.dot(mc, w_ref[0, o2:MIX_WIDTH, :], preferred_element_type=F32))
    x1_ref[...] = x1
    h2 = _rms(x1, g2_ref[...])
    hb = h2.astype(BF16)
    hn_ref[...] = hb
    lo = (h2 - hb.astype(F32)).astype(BF16)
    logits = (jnp.dot(hb, rwh_ref[...], preferred_element_type=F32)
              + jnp.dot(hb, rwl_ref[...], preferred_element_type=F32)
              + jnp.dot(lo, rwh_ref[...], preferred_element_type=F32))
    lane = lax.broadcasted_iota(jnp.int32, logits.shape, 1)
    logits = jnp.where(lane < N_EXPERTS, logits, -1e30)
    m = jnp.max(logits, axis=-1, keepdims=True)
    e = jnp.exp(logits - m)
    aff_ref[...] = e / jnp.sum(e, axis=-1, keepdims=True)


def out_proj(ya, yb, yc, x2d, branch_g, w_out_bf16, layer, norm2_g, rw_hi, rw_lo, tm=256):
    n, d = x2d.shape
    assert n % tm == 0
    row = lambda i: (i, 0)
    fixed = lambda i: (0, 0)
    return pl.pallas_call(
        _out_proj_kernel,
        grid=(n // tm,),
        in_specs=[pl.BlockSpec((tm, LRU_WIDTH), row), pl.BlockSpec((tm, NA_WIDTH), row),
                  pl.BlockSpec((tm, S5_WIDTH), row), pl.BlockSpec((tm, d), row),
                  pl.BlockSpec((1, MIX_WIDTH), fixed), pl.BlockSpec((1, MIX_WIDTH, d), lambda i: (layer, 0, 0)),
                  pl.BlockSpec((1, d), fixed), pl.BlockSpec((d, LANES), fixed), pl.BlockSpec((d, LANES), fixed)],
        out_specs=[pl.BlockSpec((tm, d), row), pl.BlockSpec((tm, d), row), pl.BlockSpec((tm, LANES), row)],
        out_shape=[jax.ShapeDtypeStruct((n, d), F32), jax.ShapeDtypeStruct((n, d), BF16),
                   jax.ShapeDtypeStruct((n, LANES), F32)],
        compiler_params=_cparams("arbitrary"),
        name="out_proj_router",
    )(ya, yb, yc, x2d, branch_g.reshape(1, MIX_WIDTH), w_out_bf16, norm2_g.reshape(1, d), rw_hi, rw_lo)


def _ffn_kernel(nf, x_ref, wg_ref, wu_ref, wd_ref, g_ref, o_ref):
    f = pl.program_id(2)
    x = x_ref[0]
    gate = jnp.dot(x, wg_ref[0, 0], preferred_element_type=F32)
    up = jnp.dot(x, wu_ref[0, 0], preferred_element_type=F32)
    h = (gate * jax.nn.sigmoid(gate) * up).astype(BF16)
    contrib = jnp.dot(h, wd_ref[0, 0], preferred_element_type=F32)

    @pl.when(f == 0)
    def _():
        o_ref[0] = contrib

    @pl.when(f > 0)
    def _():
        o_ref[0] += contrib

    @pl.when(f == nf - 1)
    def _():
        o_ref[0] = o_ref[0] * g_ref[0]


def expert_ffn(xe, wg, wu, wd, layer, g3, tm=1024, tf=512):
    ne, cap, d = xe.shape
    ff = wg.shape[3]
    tm = min(tm, cap)
    assert cap % tm == 0 and ff % tf == 0
    nf = ff // tf
    return pl.pallas_call(
        functools.partial(_ffn_kernel, nf),
        grid=(ne, cap // tm, nf),
        in_specs=[pl.BlockSpec((1, tm, d), lambda e, i, f: (e, i, 0)),
                  pl.BlockSpec((1, 1, d, tf), lambda e, i, f: (layer, e, 0, f)),
                  pl.BlockSpec((1, 1, d, tf), lambda e, i, f: (layer, e, 0, f)),
                  pl.BlockSpec((1, 1, tf, d), lambda e, i, f: (layer, e, f, 0)),
                  pl.BlockSpec((1, tm, 1), lambda e, i, f: (e, i, 0))],
        out_specs=pl.BlockSpec((1, tm, d), lambda e, i, f: (e, i, 0)),
        out_shape=jax.ShapeDtypeStruct((ne, cap, d), F32),
        compiler_params=_cparams("arbitrary", "arbitrary", "arbitrary"),
        name="expert_ffn",
    )(xe, wg, wu, wd, g3)


def _final_norm_kernel(x_ref, g_ref, o_ref):
    o_ref[...] = _rms(x_ref[...], g_ref[...])


def final_rmsnorm(x2d, g, tm=1024):
    n, d = x2d.shape
    tm = min(tm, n)
    assert n % tm == 0
    return pl.pallas_call(
        _final_norm_kernel,
        grid=(n // tm,),
        in_specs=[pl.BlockSpec((tm, d), lambda i: (i, 0)), pl.BlockSpec((1, d), lambda i: (0, 0))],
        out_specs=pl.BlockSpec((tm, d), lambda i: (i, 0)),
        out_shape=jax.ShapeDtypeStruct((n, d), F32),
        compiler_params=_cparams("arbitrary"),
        name="final_rmsnorm",
    )(x2d, g.reshape(1, d))


def prep_layer(norm1_g, conv_w, conv_b, gate_a_w, gate_a_b, gate_x_w, gate_x_b, lru_lambda, rpb,
               s5_lambda_re, s5_lambda_im, s5_log_dt, s5_b_re, s5_b_im, s5_c_re, s5_c_im, s5_d, glu_w, glu_b,
               branch_g, norm2_g, router_w, s5_levels):
    d = router_w.shape[0]
    rw = jnp.zeros((d, LANES), F32).at[:, :N_EXPERTS].set(router_w.astype(F32))
    rw_hi = rw.astype(BF16)
    rw_lo = (rw - rw_hi.astype(F32)).astype(BF16)
    return dict(
        norm1_g=norm1_g, conv_w=conv_w.astype(F32), conv_b=conv_b.astype(F32),
        wa=gate_a_w.astype(BF16), wx=gate_x_w.astype(BF16), ba=gate_a_b.astype(F32), bx=gate_x_b.astype(F32),
        c8=-LRU_C * jax.nn.softplus(-lru_lambda.astype(F32)),
        bias_pat=na_bias_patterns(rpb),
        s5w=s5_prep(s5_lambda_re, s5_lambda_im, s5_log_dt, s5_b_re, s5_b_im, s5_c_re, s5_c_im, s5_levels),
        s5_d=s5_d.astype(F32), glu_w=glu_w.astype(BF16), glu_b=glu_b.astype(F32),
        branch_g=branch_g.astype(F32), norm2_g=norm2_g.astype(F32), rw_hi=rw_hi, rw_lo=rw_lo)


def trunk_layer(x, lw, big, layer):
    bn, L, d = x.shape
    n = bn * L
    x2d = x.reshape(n, d)
    proj = in_proj(x2d, lw['norm1_g'], big['w_in'], layer)
    proj3 = proj.reshape(bn, L, IN_WIDTH)
    hf = lru_scan(proj3, lw['conv_w'], lw['conv_b'], lw['wa'][0], lw['wx'][0], lw['ba'][0], lw['bx'][0],
                  lw['c8'][0], rev=False)
    ya = lru_scan(proj3, lw['conv_w'], lw['conv_b'], lw['wa'][1], lw['wx'][1], lw['ba'][1], lw['bx'][1],
                  lw['c8'][1], rev=True, hf=hf)
    yb = na_attention(proj3, lw['bias_pat'])
    u = proj[:, IN_WIDTH - S5_WIDTH:]
    yc = s5_mixer(u, lw['s5w'], lw['s5_d'], lw['glu_w'], lw['glu_b'], L)
    x1, hn, aff = out_proj(ya.reshape(n, LRU_WIDTH), yb.reshape(n, NA_WIDTH), yc, x2d, lw['branch_g'],
                           big['w_out'], layer, lw['norm2_g'], lw['rw_hi'], lw['rw_lo'])
    cap = max(1, EC_CAPACITY * n // N_EXPERTS)
    g, idx = lax.top_k(aff[:, :N_EXPERTS].T, cap)
    xe = hn[idx]
    ye = expert_ffn(xe, big['w_gate'], big['w_up'], big['w_down'], layer, g[..., None])
    x2 = x1.at[idx.reshape(-1)].add(ye.reshape(-1, d))
    return x2.reshape(bn, L, d)


def kernel(x_prompt, x_sample, norm1_g, w_in, conv_w, conv_b, gate_a_w, gate_a_b, gate_x_w, gate_x_b,
           lru_lambda, rpb, s5_lambda_re, s5_lambda_im, s5_log_dt, s5_b_re, s5_b_im, s5_c_re, s5_c_im,
           s5_d, glu_w, glu_b, branch_g, w_out, norm2_g, router_w, w_gate, w_up, w_down, final_g):
    xp = x_prompt
    xs = x_sample
    max_len = max(xp.shape[1], xs.shape[1])
    s5_levels = int(max_len // S5_CHUNK).bit_length() - 1
    big = dict(w_in=w_in.astype(BF16), w_out=w_out.astype(BF16), w_gate=w_gate.astype(BF16),
               w_up=w_up.astype(BF16), w_down=w_down.astype(BF16))
    for l in range(DEPTH):
        lw = prep_layer(norm1_g[l], conv_w[l], conv_b[l], gate_a_w[l], gate_a_b[l], gate_x_w[l],
                        gate_x_b[l], lru_lambda[l], rpb[l], s5_lambda_re[l], s5_lambda_im[l], s5_log_dt[l],
                        s5_b_re[l], s5_b_im[l], s5_c_re[l], s5_c_im[l], s5_d[l], glu_w[l], glu_b[l],
                        branch_g[l], norm2_g[l], router_w[l], s5_levels)
        xp = trunk_layer(xp, lw, big, l)
        xs = trunk_layer(xs, lw, big, l)
    d = xp.shape[-1]
    y_prompt = final_rmsnorm(xp.reshape(-1, d), final_g).reshape(xp.shape)
    y_sample = final_rmsnorm(xs.reshape(-1, d), final_g).reshape(xs.shape)
    return (y_prompt, y_sample)
```
